```python
import jax, jax.numpy as jnp
from jax import lax
import numpy as np

D_MODEL = 2048
BATCH = 4
SEQ = 2048
DEPTH = 4
DEC_BATCH = 128
DEC_SEQ = 1
PAST_LEN = 16384
PAGE_SIZE = 128

N_MIXERS = 4
N_A = (DEPTH + 3) // 4
N_B = (DEPTH + 2) // 4
N_C = (DEPTH + 1) // 4
N_D = DEPTH // 4
D_FF = 4 * D_MODEL
CONV_A_WIDTH = 31
POOL_WINDOWS = (2, 4, 8, 16)
N_POOL_GROUPS = 4
POOL_GROUP = D_MODEL // N_POOL_GROUPS
POOL_STATE = max(POOL_WINDOWS) - 1
CONV_C_WIDTH = 3
CHUNK = 128
N_SG_HEADS = 8
SG_HEAD_DIM = D_MODEL // N_SG_HEADS
RMS_EPS = 1e-6
LN_EPS = 1e-5

kernel_name = "hybrid_conv_pool_gmlp_decoder_step"


def rmsnorm(x, g):
    xf = x.astype(jnp.float32)
    r = lax.rsqrt(jnp.mean(xf * xf, axis=-1, keepdims=True) + RMS_EPS)
    return (xf * r * g.astype(jnp.float32)).astype(x.dtype)


def layernorm(x, g, b):
    xf = x.astype(jnp.float32)
    mu = jnp.mean(xf, axis=-1, keepdims=True)
    xc = xf - mu
    var = jnp.mean(xc * xc, axis=-1, keepdims=True)
    return (xc * lax.rsqrt(var + LN_EPS) * g.astype(jnp.float32) + b.astype(jnp.float32)).astype(x.dtype)


def causal_dwconv(x, prefix, w):
    xp = jnp.concatenate([prefix.astype(x.dtype), x], axis=1)
    out = lax.conv_general_dilated(
        xp, w[:, None, :].astype(x.dtype), window_strides=(1,), padding='VALID',
        dimension_numbers=('NWC', 'WIO', 'NWC'), feature_group_count=x.shape[-1])
    return out, xp[:, -(w.shape[0] - 1):]


def conformer_conv(h, prefix, w_pw1, b_pw1, w_dw, b_dw, ln_g, ln_b, w_pw2):
    a = h @ w_pw1 + b_pw1
    g = a[..., :D_MODEL] * jax.nn.sigmoid(a[..., D_MODEL:])
    c, new_state = causal_dwconv(g, prefix, w_dw)
    c = layernorm(c + b_dw, ln_g, ln_b)
    return jax.nn.silu(c) @ w_pw2, new_state


def multiscale_pool(h, prefix, p0, w_grp, scale):
    B, L, _ = h.shape
    hp = jnp.concatenate([prefix.astype(h.dtype), h], axis=1).astype(jnp.float32)
    c = jnp.cumsum(hp, axis=1)
    c = jnp.concatenate([jnp.zeros_like(c[:, :1]), c], axis=1)
    pos = p0 + jnp.arange(L)
    groups = []
    for gi, w in enumerate(POOL_WINDOWS):
        sl = slice(gi * POOL_GROUP, (gi + 1) * POOL_GROUP)
        end = c[:, POOL_STATE + 1:POOL_STATE + 1 + L, sl]
        start = c[:, POOL_STATE + 1 - w:POOL_STATE + 1 - w + L, sl]
        cnt = jnp.minimum(pos + 1, w).astype(jnp.float32)[None, :, None]
        groups.append((end - start) / cnt)
    pooled = jnp.stack(groups, axis=2)
    diff = (pooled - hp[:, POOL_STATE:].reshape(B, L, N_POOL_GROUPS, POOL_GROUP)).astype(h.dtype)
    y = jnp.einsum('blgc,gcd->blgd', diff, w_grp).reshape(B, L, D_MODEL) * scale
    return y, hp[:, -POOL_STATE:].astype(h.dtype)


def short_gated_conv(h, prefix, w_in, w_conv, w_out):
    bcx = h @ w_in
    b_gate, c_gate, xv = jnp.split(bcx, 3, axis=-1)
    y, new_state = causal_dwconv(c_gate * xv, prefix, w_conv)
    return (b_gate * y) @ w_out, new_state


def chunk_spatial_gating(h, w_uv, ln_g, ln_b, w_s, b_s, w_o):
    B, L, _ = h.shape
    z = jax.nn.gelu(h @ w_uv)
    u, v = z[..., :D_MODEL], z[..., D_MODEL:]
    v = layernorm(v, ln_g, ln_b)
    n_chunks = -(-L // CHUNK)
    pad = n_chunks * CHUNK - L
    vc = jnp.pad(v, ((0, 0), (0, pad), (0, 0))).reshape(B, n_chunks, CHUNK, N_SG_HEADS, SG_HEAD_DIM)
    mask = jnp.tril(jnp.ones((CHUNK, CHUNK), dtype=bool))
    ws = jnp.where(mask[None], w_s, jnp.zeros_like(w_s))
    mixed = jnp.einsum('gts,bcsgd->bctgd', ws, vc) + b_s.T[None, None, :, :, None]
    mixed = mixed.reshape(B, n_chunks * CHUNK, D_MODEL)[:, :L]
    return (u * mixed) @ w_o, v


def squared_relu_mlp(h, w1, w2):
    return jnp.square(jax.nn.relu(h @ w1)) @ w2


def setup_inputs(seed: int = 0) -> dict:
    key = jax.random.key(seed)
    ks = iter(jax.random.split(key, 40))
    nrm = lambda shape, s: jax.random.normal(next(ks), shape, jnp.float32) * s
    D = D_MODEL
    return {
        "x_prompt": nrm((BATCH, SEQ, D), 1.0),
        "x_sample": nrm((DEC_BATCH, DEC_SEQ, D), 1.0),
        "state_conv_a": nrm((N_A, DEC_BATCH, CONV_A_WIDTH - 1, D), 0.5),
        "state_pool": nrm((N_B, DEC_BATCH, POOL_STATE, D), 1.0),
        "state_short_conv": nrm((N_C, DEC_BATCH, CONV_C_WIDTH - 1, D), 0.5),
        "norm_mix": 1.0 + nrm((DEPTH, D), 0.02),
        "norm_mlp": 1.0 + nrm((DEPTH, D), 0.02),
        "norm_final": 1.0 + nrm((D,), 0.02),
        "a_w_pw1": nrm((N_A, D, 2 * D), D ** -0.5),
        "a_b_pw1": nrm((N_A, 2 * D), 0.02),
        "a_w_dw": nrm((N_A, CONV_A_WIDTH, D), CONV_A_WIDTH ** -0.5),
        "a_b_dw": nrm((N_A, D), 0.02),
        "a_ln_g": 1.0 + nrm((N_A, D), 0.02),
        "a_ln_b": nrm((N_A, D), 0.02),
        "a_w_pw2": nrm((N_A, D, D), D ** -0.5),
        "b_w_grp": nrm((N_B, N_POOL_GROUPS, POOL_GROUP, POOL_GROUP), POOL_GROUP ** -0.5),
        "b_scale": 0.5 + nrm((N_B, D), 0.05),
        "c_w_in": nrm((N_C, D, 3 * D), D ** -0.5),
        "c_w_conv": nrm((N_C, CONV_C_WIDTH, D), CONV_C_WIDTH ** -0.5),
        "c_w_out": nrm((N_C, D, D), D ** -0.5),
        "d_w_uv": nrm((N_D, D, 2 * D), D ** -0.5),
        "d_ln_g": 1.0 + nrm((N_D, D), 0.02),
        "d_ln_b": nrm((N_D, D), 0.02),
        "d_w_s": nrm((N_D, N_SG_HEADS, CHUNK, CHUNK), CHUNK ** -0.5),
        "d_b_s": 1.0 + nrm((N_D, N_SG_HEADS, CHUNK), 0.02),
        "d_w_o": nrm((N_D, D, D), D ** -0.5),
        "mlp_w1": nrm((DEPTH, D, D_FF), D ** -0.5),
        "mlp_w2": nrm((DEPTH, D_FF, D), 0.5 * D_FF ** -0.5),
    }


def reference(x_prompt, x_sample, state_conv_a, state_pool, state_short_conv,
              norm_mix, norm_mlp, norm_final,
              a_w_pw1, a_b_pw1, a_w_dw, a_b_dw, a_ln_g, a_ln_b, a_w_pw2,
              b_w_grp, b_scale,
              c_w_in, c_w_conv, c_w_out,
              d_w_uv, d_ln_g, d_ln_b, d_w_s, d_b_s, d_w_o,
              mlp_w1, mlp_w2):

    def trunk(x, p0, pref_a, pref_b, pref_c):
        st_a, st_b, st_c, st_d = [], [], [], []
        for i in range(DEPTH):
            m, j = i % N_MIXERS, i // N_MIXERS
            h = rmsnorm(x, norm_mix[i])
            if m == 0:
                y, s = conformer_conv(h, pref_a[j], a_w_pw1[j], a_b_pw1[j], a_w_dw[j], a_b_dw[j],
                                      a_ln_g[j], a_ln_b[j], a_w_pw2[j])
                st_a.append(s)
            elif m == 1:
                y, s = multiscale_pool(h, pref_b[j], p0, b_w_grp[j], b_scale[j])
                st_b.append(s)
            elif m == 2:
                y, s = short_gated_conv(h, pref_c[j], c_w_in[j], c_w_conv[j], c_w_out[j])
                st_c.append(s)
            else:
                y, s = chunk_spatial_gating(h, d_w_uv[j], d_ln_g[j], d_ln_b[j], d_w_s[j], d_b_s[j], d_w_o[j])
                st_d.append(s)
            x = x + y
            x = x + squared_relu_mlp(rmsnorm(x, norm_mlp[i]), mlp_w1[i], mlp_w2[i])
        return rmsnorm(x, norm_final), st_a, st_b, st_c, st_d

    dt = x_prompt.dtype
    zeros_a = [jnp.zeros((BATCH, CONV_A_WIDTH - 1, D_MODEL), dt)] * N_A
    zeros_b = [jnp.zeros((BATCH, POOL_STATE, D_MODEL), dt)] * N_B
    zeros_c = [jnp.zeros((BATCH, CONV_C_WIDTH - 1, D_MODEL), dt)] * N_C
    y_prompt, pa, pb, pc, _ = trunk(x_prompt, 0, zeros_a, zeros_b, zeros_c)
    y_sample, sa, sb, sc, sd = trunk(x_sample, PAST_LEN, state_conv_a, state_pool, state_short_conv)

    return (y_prompt, y_sample,
            jnp.stack(pa), jnp.stack(sa),
            jnp.stack(pb), jnp.stack(sb),
            jnp.stack(pc), jnp.stack(sc),
            jnp.stack(sd))
```

```python
import functools

import jax
import jax.numpy as jnp
from jax import lax
from jax.experimental import pallas as pl
from jax.experimental.pallas import tpu as pltpu

D_MODEL = 2048
D_FF = 4 * D_MODEL
DEPTH = 4
CONV_A_WIDTH = 31
POOL_WINDOWS = (2, 4, 8, 16)
POOL_GROUP = D_MODEL // len(POOL_WINDOWS)
POOL_STATE = max(POOL_WINDOWS) - 1
CONV_C_WIDTH = 3
CHUNK = 128
N_SG_HEADS = 8
SG_HEAD_DIM = D_MODEL // N_SG_HEADS
PAST_LEN = 16384
RMS_EPS = 1e-6
LN_EPS = 1e-5

SUBLANES = 8
VMEM_LIMIT_BYTES = 56 * 1024 * 1024
HALO = 32

F32 = jnp.float32
BF16 = jnp.bfloat16


def _params(*sem):
    return pltpu.CompilerParams(dimension_semantics=sem, vmem_limit_bytes=VMEM_LIMIT_BYTES)


def _for_chunks(n, chunk, body):
    assert n % chunk == 0
    if n == chunk:
        body(0)
    else:
        def step(i, c):
            body(pl.multiple_of(i * chunk, chunk))
            return c
        lax.fori_loop(0, n // chunk, step, 0)


def _rms_rows(x, g):
    r = lax.rsqrt(jnp.mean(x * x, axis=-1, keepdims=True) + RMS_EPS)
    return x * r * g


def _ln_rows(x, g, b):
    mu = jnp.mean(x, axis=-1, keepdims=True)
    xc = x - mu
    var = jnp.mean(xc * xc, axis=-1, keepdims=True)
    return xc * lax.rsqrt(var + LN_EPS) * g + b


def _row_chunk(tm):
    for c in (64, 32, 16, 8):
        if tm % c == 0:
            return c
    raise ValueError(tm)


def _mm_parts_kernel(*refs, n_parts, n_out, has_bias, epilogue, tm):
    x_ref, g_ref = refs[0], refs[1]
    w_refs = refs[2:2 + n_parts]
    pos = 2 + n_parts
    b_refs = refs[pos:pos + n_parts] if has_bias else ()
    pos += n_parts if has_bias else 0
    o_refs = refs[pos:pos + n_out]
    xn_ref = refs[pos + n_out]

    @pl.when(pl.program_id(1) == 0)
    def _():
        rc = _row_chunk(tm)

        def body(r0):
            x = x_ref[pl.ds(r0, rc), :]
            xn_ref[pl.ds(r0, rc), :] = _rms_rows(x, g_ref[...]).astype(BF16)
        _for_chunks(tm, rc, body)

    xn = xn_ref[...]
    parts = []
    for p in range(n_parts):
        a = jnp.dot(xn, w_refs[p][...], preferred_element_type=F32)
        if has_bias:
            a = a + b_refs[p][...]
        parts.append(a)
    outs = epilogue(*parts)
    for o_ref, o in zip(o_refs, outs):
        o_ref[...] = o.astype(o_ref.dtype)


def _mm_parts(x, g, w, bias, n_parts, out_dtypes, epilogue, tm, tn):
    m = x.shape[0]
    nj = D_MODEL // tn
    in_specs = [pl.BlockSpec((tm, D_MODEL), lambda i, j: (i, 0)),
                pl.BlockSpec((1, D_MODEL), lambda i, j: (0, 0))]
    args = [x, g.reshape(1, D_MODEL)]
    for p in range(n_parts):
        in_specs.append(pl.BlockSpec((D_MODEL, tn), lambda i, j, p=p: (0, p * nj + j)))
        args.append(w)
    if bias is not None:
        b2 = bias.reshape(1, n_parts * D_MODEL)
        for p in range(n_parts):
            in_specs.append(pl.BlockSpec((1, tn), lambda i, j, p=p: (0, p * nj + j)))
            args.append(b2)
    kern = functools.partial(_mm_parts_kernel, n_parts=n_parts, n_out=len(out_dtypes),
                             has_bias=bias is not None, epilogue=epilogue, tm=tm)
    return pl.pallas_call(
        kern,
        grid=(m // tm, nj),
        in_specs=in_specs,
        out_specs=[pl.BlockSpec((tm, tn), lambda i, j: (i, j)) for _ in out_dtypes],
        out_shape=[jax.ShapeDtypeStruct((m, D_MODEL), dt) for dt in out_dtypes],
        scratch_shapes=[pltpu.VMEM((tm, D_MODEL), BF16)],
        compiler_params=_params("parallel", "arbitrary"),
    )(*args)


def _glu_epilogue(a, b):
    return (a * jax.nn.sigmoid(b),)


def _gelu_epilogue(a, b):
    return (jax.nn.gelu(a), jax.nn.gelu(b))


def _gate3_epilogue(b, c, xv):
    return (b, c * xv)


def _mm_res_kernel(a_ref, w_ref, x_ref, o_ref):
    o_ref[...] = x_ref[...] + jnp.dot(a_ref[...], w_ref[...], preferred_element_type=F32)


def _mm_res(a, w, x, tm):
    m = a.shape[0]
    return pl.pallas_call(
        _mm_res_kernel,
        grid=(m // tm,),
        in_specs=[pl.BlockSpec((tm, D_MODEL), lambda i: (i, 0)),
                  pl.BlockSpec((D_MODEL, D_MODEL), lambda i: (0, 0)),
                  pl.BlockSpec((tm, D_MODEL), lambda i: (i, 0))],
        out_specs=pl.BlockSpec((tm, D_MODEL), lambda i: (i, 0)),
        out_shape=jax.ShapeDtypeStruct((m, D_MODEL), F32),
        compiler_params=_params("parallel"),
    )(a, w, x)


def _mlp_kernel(*refs, tm, final):
    if final:
        x_ref, g_ref, w1_ref, w2_ref, gf_ref, o_ref, xn_ref = refs
    else:
        x_ref, g_ref, w1_ref, w2_ref, o_ref, xn_ref = refs
    j = pl.program_id(1)
    rc = _row_chunk(tm)

    @pl.when(j == 0)
    def _():
        def body(r0):
            x = x_ref[pl.ds(r0, rc), :]
            xn_ref[pl.ds(r0, rc), :] = _rms_rows(x, g_ref[...]).astype(BF16)
            o_ref[pl.ds(r0, rc), :] = x
        _for_chunks(tm, rc, body)

    h = jnp.dot(xn_ref[...], w1_ref[...], preferred_element_type=F32)
    h = jnp.square(jnp.maximum(h, 0.0)).astype(BF16)
    o_ref[...] += jnp.dot(h, w2_ref[...], preferred_element_type=F32)

    if final:
        @pl.when(j == pl.num_programs(1) - 1)
        def _():
            def body(r0):
                o_ref[pl.ds(r0, rc), :] = _rms_rows(o_ref[pl.ds(r0, rc), :], gf_ref[...])
            _for_chunks(tm, rc, body)


def _mlp(x, g, w1, w2, tm, tf, final_g=None):
    m = x.shape[0]
    final = final_g is not None
    in_specs = [pl.BlockSpec((tm, D_MODEL), lambda i, j: (i, 0)),
                pl.BlockSpec((1, D_MODEL), lambda i, j: (0, 0)),
                pl.BlockSpec((D_MODEL, tf), lambda i, j: (0, j)),
                pl.BlockSpec((tf, D_MODEL), lambda i, j: (j, 0))]
    args = [x, g.reshape(1, D_MODEL), w1, w2]
    if final:
        in_specs.append(pl.BlockSpec((1, D_MODEL), lambda i, j: (0, 0)))
        args.append(final_g.reshape(1, D_MODEL))
    return pl.pallas_call(
        functools.partial(_mlp_kernel, tm=tm, final=final),
        grid=(m // tm, D_FF // tf),
        in_specs=in_specs,
        out_specs=pl.BlockSpec((tm, D_MODEL), lambda i, j: (i, 0)),
        out_shape=jax.ShapeDtypeStruct((m, D_MODEL), F32),
        scratch_shapes=[pltpu.VMEM((tm, D_MODEL), BF16)],
        compiler_params=_params("parallel", "arbitrary"),
    )(*args)


CONV_ROWS = 64
CONV_LANES = 256


def _conv_a_kernel(g_ref, w_ref, bdw_ref, lng_ref, lnb_ref, o_ref, ext_ref, c_ref, *, tt):
    t = pl.program_id(1)

    @pl.when(t == 0)
    def _():
        ext_ref[pl.ds(0, HALO), :] = jnp.zeros((HALO, D_MODEL), F32)

    @pl.when(t > 0)
    def _():
        ext_ref[pl.ds(0, HALO), :] = ext_ref[pl.ds(tt, HALO), :]

    def copy_in(r0):
        ext_ref[pl.ds(HALO + r0, CONV_ROWS), :] = g_ref[0, pl.ds(r0, CONV_ROWS), :]
    _for_chunks(tt, CONV_ROWS, copy_in)

    off = HALO - (CONV_A_WIDTH - 1)
    win = CONV_ROWS + HALO
    for c0 in range(0, D_MODEL, CONV_LANES):
        lanes = pl.ds(c0, CONV_LANES)

        def rows(r0, lanes=lanes):
            w = ext_ref[pl.ds(r0, win), lanes]
            acc = jnp.zeros((CONV_ROWS, CONV_LANES), F32)
            for s in range(SUBLANES):
                ws = w if s == 0 else pltpu.roll(w, win - s, 0)
                for k in range(CONV_A_WIDTH):
                    if (off + k) % SUBLANES == s:
                        a = off + k - s
                        acc = acc + ws[a:a + CONV_ROWS] * w_ref[pl.ds(k, 1), lanes]
            c_ref[pl.ds(r0, CONV_ROWS), lanes] = acc
        _for_chunks(tt, CONV_ROWS, rows)

    rc = 32

    def norm(r0):
        c = c_ref[pl.ds(r0, rc), :] + bdw_ref[...]
        y = _ln_rows(c, lng_ref[...], lnb_ref[...])
        o_ref[0, pl.ds(r0, rc), :] = (y * jax.nn.sigmoid(y)).astype(BF16)
    _for_chunks(tt, rc, norm)


def _conv_a_prompt(g3, w_dw, b_dw, ln_g, ln_b, tt):
    b, l, _ = g3.shape
    wpad = jnp.zeros((32, D_MODEL), F32).at[:CONV_A_WIDTH].set(w_dw)
    vec = lambda v: v.reshape(1, D_MODEL)
    vspec = pl.BlockSpec((1, D_MODEL), lambda i, t: (0, 0))
    return pl.pallas_call(
        functools.partial(_conv_a_kernel, tt=tt),
        grid=(b, l // tt),
        in_specs=[pl.BlockSpec((1, tt, D_MODEL), lambda i, t: (i, t, 0)),
                  pl.BlockSpec((32, D_MODEL), lambda i, t: (0, 0)),
                  vspec, vspec, vspec],
        out_specs=pl.BlockSpec((1, tt, D_MODEL), lambda i, t: (i, t, 0)),
        out_shape=jax.ShapeDtypeStruct((b, l, D_MODEL), BF16),
        scratch_shapes=[pltpu.VMEM((tt + HALO, D_MODEL), F32),
                        pltpu.VMEM((tt, D_MODEL), F32)],
        compiler_params=_params("arbitrary", "arbitrary"),
    )(g3, wpad, vec(b_dw), vec(ln_g), vec(ln_b))


def _conv_a_sample_kernel(st_ref, g_ref, w_ref, bdw_ref, lng_ref, lnb_ref, o_ref):
    acc = g_ref[...] * w_ref[pl.ds(CONV_A_WIDTH - 1, 1), :]
    for k in range(CONV_A_WIDTH - 1):
        acc = acc + st_ref[:, k, :] * w_ref[pl.ds(k, 1), :]
    y = _ln_rows(acc + bdw_ref[...], lng_ref[...], lnb_ref[...])
    o_ref[...] = (y * jax.nn.sigmoid(y)).astype(BF16)


def _conv_a_sample(state, g, w_dw, b_dw, ln_g, ln_b, tb):
    n = g.shape[0]
    wpad = jnp.zeros((32, D_MODEL), F32).at[:CONV_A_WIDTH].set(w_dw)
    vec = lambda v: v.reshape(1, D_MODEL)
    vspec = pl.BlockSpec((1, D_MODEL), lambda i: (0, 0))
    return pl.pallas_call(
        _conv_a_sample_kernel,
        grid=(n // tb,),
        in_specs=[pl.BlockSpec((tb, CONV_A_WIDTH - 1, D_MODEL), lambda i: (i, 0, 0)),
                  pl.BlockSpec((tb, D_MODEL), lambda i: (i, 0)),
                  pl.BlockSpec((32, D_MODEL), lambda i: (0, 0)),
                  vspec, vspec, vspec],
        out_specs=pl.BlockSpec((tb, D_MODEL), lambda i: (i, 0)),
        out_shape=jax.ShapeDtypeStruct((n, D_MODEL), BF16),
        compiler_params=_params("parallel"),
    )(state, g, wpad, vec(b_dw), vec(ln_g), vec(ln_b))


def _pool_group_matmul(diff_of_group, w_ref, sc_ref, x, o_store):
    for gi in range(len(POOL_WINDOWS)):
        sl = slice(gi * POOL_GROUP, (gi + 1) * POOL_GROUP)
        y = jnp.dot(diff_of_group(gi).astype(BF16), w_ref[gi], preferred_element_type=F32)
        o_store(sl, x[:, sl] + y * sc_ref[:, sl])


def _pool_prompt_kernel(x_ref, g_ref, w_ref, sc_ref, o_ref, hs_ref, ext_ref, diff_ref, *, tt):
    t = pl.program_id(1)
    H = 16

    @pl.when(t == 0)
    def _():
        ext_ref[pl.ds(0, H), :] = jnp.zeros((H, D_MODEL), F32)

    @pl.when(t > 0)
    def _():
        ext_ref[pl.ds(0, H), :] = ext_ref[pl.ds(tt, H), :]

    rc = 32

    def norm(r0):
        ext_ref[pl.ds(H + r0, rc), :] = _rms_rows(x_ref[0, pl.ds(r0, rc), :], g_ref[...])
    _for_chunks(tt, rc, norm)

    hs_ref[0, 0] = ext_ref[pl.ds(tt, H), :]

    pr = 64
    win = pr + H
    for gi, w in enumerate(POOL_WINDOWS):
        lanes = pl.ds(gi * POOL_GROUP, POOL_GROUP)

        def rows(r0, w=w, lanes=lanes):
            s = ext_ref[pl.ds(r0, win), lanes]
            cur = s[H:]
            d = 1
            while d < w:
                s = s + pltpu.roll(s, d, 0)
                d *= 2
            pos = t * tt + r0 + lax.broadcasted_iota(jnp.int32, (pr, 1), 0)
            cnt = jnp.minimum(pos + 1, w).astype(F32)
            diff_ref[pl.ds(r0, pr), lanes] = (s[H:] / cnt - cur).astype(BF16)
        _for_chunks(tt, pr, rows)

    for gi in range(len(POOL_WINDOWS)):
        sl = slice(gi * POOL_GROUP, (gi + 1) * POOL_GROUP)
        y = jnp.dot(diff_ref[:, sl], w_ref[gi], preferred_element_type=F32)
        o_ref[0, :, sl] = x_ref[0, :, sl] + y * sc_ref[:, sl]


def _pool_prompt(x3, g, w_grp, scale, tt):
    b, l, _ = x3.shape
    nt = l // tt
    vspec = pl.BlockSpec((1, D_MODEL), lambda i, t: (0, 0))
    return pl.pallas_call(
        functools.partial(_pool_prompt_kernel, tt=tt),
        grid=(b, nt),
        in_specs=[pl.BlockSpec((1, tt, D_MODEL), lambda i, t: (i, t, 0)),
                  vspec,
                  pl.BlockSpec((len(POOL_WINDOWS), POOL_GROUP, POOL_GROUP), lambda i, t: (0, 0, 0)),
                  vspec],
        out_specs=[pl.BlockSpec((1, tt, D_MODEL), lambda i, t: (i, t, 0)),
                   pl.BlockSpec((1, 1, 16, D_MODEL), lambda i, t: (i, t, 0, 0))],
        out_shape=[jax.ShapeDtypeStruct((b, l, D_MODEL), F32),
                   jax.ShapeDtypeStruct((b, nt, 16, D_MODEL), F32)],
        scratch_shapes=[pltpu.VMEM((tt + 16, D_MODEL), F32), pltpu.VMEM((tt, D_MODEL), BF16)],
        compiler_params=_params("arbitrary", "arbitrary"),
    )(x3, g.reshape(1, D_MODEL), w_grp, scale.reshape(1, D_MODEL))


def _pool_sample_kernel(x_ref, st_ref, g_ref, w_ref, sc_ref, o_ref, h_ref):
    x = x_ref[...]
    h = _rms_rows(x, g_ref[...])
    h_ref[...] = h

    def diff_of_group(gi):
        w = POOL_WINDOWS[gi]
        sl = slice(gi * POOL_GROUP, (gi + 1) * POOL_GROUP)
        cur = h[:, sl]
        s = cur
        for d in range(1, w):
            s = s + st_ref[:, POOL_STATE - d, sl]
        cnt = float(min(PAST_LEN + 1, w))
        return s / cnt - cur

    def o_store(sl, v):
        o_ref[:, sl] = v

    _pool_group_matmul(diff_of_group, w_ref, sc_ref, x, o_store)


def _pool_sample(x, state, g, w_grp, scale, tb):
    n = x.shape[0]
    vspec = pl.BlockSpec((1, D_MODEL), lambda i: (0, 0))
    row = pl.BlockSpec((tb, D_MODEL), lambda i: (i, 0))
    return pl.pallas_call(
        _pool_sample_kernel,
        grid=(n // tb,),
        in_specs=[row,
                  pl.BlockSpec((tb, POOL_STATE, D_MODEL), lambda i: (i, 0, 0)),
                  vspec,
                  pl.BlockSpec((len(POOL_WINDOWS), POOL_GROUP, POOL_GROUP), lambda i: (0, 0, 0)),
                  vspec],
        out_specs=[row, row],
        out_shape=[jax.ShapeDtypeStruct((n, D_MODEL), F32),
                   jax.ShapeDtypeStruct((n, D_MODEL), F32)],
        compiler_params=_params("parallel"),
    )(x, state, g.reshape(1, D_MODEL), w_grp, scale.reshape(1, D_MODEL))


def _conv_c_prompt_kernel(b_ref, cx_ref, w_ref, o_ref, ext_ref, *, tt):
    t = pl.program_id(1)
    H = SUBLANES

    @pl.when(t == 0)
    def _():
        ext_ref[pl.ds(0, H), :] = jnp.zeros((H, D_MODEL), F32)

    @pl.when(t > 0)
    def _():
        ext_ref[pl.ds(0, H), :] = ext_ref[pl.ds(tt, H), :]

    rc = 64

    def copy_in(r0):
        ext_ref[pl.ds(H + r0, rc), :] = cx_ref[0, pl.ds(r0, rc), :]
    _for_chunks(tt, rc, copy_in)

    win = rc + H
    cl = 512
    for c0 in range(0, D_MODEL, cl):
        lanes = pl.ds(c0, cl)

        def rows(r0, lanes=lanes):
            w = ext_ref[pl.ds(r0, win), lanes]
            y = (pltpu.roll(w, 2, 0)[H:] * w_ref[pl.ds(0, 1), lanes]
                 + pltpu.roll(w, 1, 0)[H:] * w_ref[pl.ds(1, 1), lanes]
                 + w[H:] * w_ref[pl.ds(2, 1), lanes])
            o_ref[0, pl.ds(r0, rc), lanes] = (b_ref[0, pl.ds(r0, rc), lanes] * y).astype(BF16)
        _for_chunks(tt, rc, rows)


def _conv_c_prompt(b3, cx3, w_conv, tt):
    b, l, _ = b3.shape
    wpad = jnp.zeros((SUBLANES, D_MODEL), F32).at[:CONV_C_WIDTH].set(w_conv)
    blk = pl.BlockSpec((1, tt, D_MODEL), lambda i, t: (i, t, 0))
    return pl.pallas_call(
        functools.partial(_conv_c_prompt_kernel, tt=tt),
        grid=(b, l // tt),
        in_specs=[blk, blk, pl.BlockSpec((SUBLANES, D_MODEL), lambda i, t: (0, 0))],
        out_specs=blk,
        out_shape=jax.ShapeDtypeStruct((b, l, D_MODEL), BF16),
        scratch_shapes=[pltpu.VMEM((tt + SUBLANES, D_MODEL), F32)],
        compiler_params=_params("arbitrary", "arbitrary"),
    )(b3, cx3, wpad)


def _conv_c_sample_kernel(b_ref, cx_ref, st_ref, w_ref, o_ref):
    y = (st_ref[:, 0, :] * w_ref[pl.ds(0, 1), :] + st_ref[:, 1, :] * w_ref[pl.ds(1, 1), :]
         + cx_ref[...] * w_ref[pl.ds(2, 1), :])
    o_ref[...] = (b_ref[...] * y).astype(BF16)


def _conv_c_sample(bg, cx, state, w_conv):
    n = bg.shape[0]
    wpad = jnp.zeros((SUBLANES, D_MODEL), F32).at[:CONV_C_WIDTH].set(w_conv)
    row = pl.BlockSpec((n, D_MODEL), lambda i: (0, 0))
    return pl.pallas_call(
        _conv_c_sample_kernel,
        grid=(1,),
        in_specs=[row, row,
                  pl.BlockSpec((n, CONV_C_WIDTH - 1, D_MODEL), lambda i: (0, 0, 0)),
                  pl.BlockSpec((SUBLANES, D_MODEL), lambda i: (0, 0))],
        out_specs=row,
        out_shape=jax.ShapeDtypeStruct((n, D_MODEL), BF16),
        compiler_params=_params("arbitrary"),
    )(bg, cx, state, wpad)


def _sg_prompt_kernel(u_ref, v_ref, lng_ref, lnb_ref, ws_ref, bs_ref, wo_ref, x_ref, o_ref,
                      vn_ref, gt_ref, *, tm):
    rc = 32

    def norm(r0):
        vn_ref[pl.ds(r0, rc), :] = _ln_rows(v_ref[pl.ds(r0, rc), :], lng_ref[...],
                                            lnb_ref[...]).astype(BF16)
    _for_chunks(tm, rc, norm)

    row = lax.broadcasted_iota(jnp.int32, (CHUNK, CHUNK), 0)
    col = lax.broadcasted_iota(jnp.int32, (CHUNK, CHUNK), 1)
    for hd in range(N_SG_HEADS):
        ws = jnp.where(col <= row, ws_ref[hd], 0.0).astype(BF16)
        lanes = pl.ds(hd * SG_HEAD_DIM, SG_HEAD_DIM)

        def chunk(r0, ws=ws, lanes=lanes):
            mixed = jnp.dot(ws, vn_ref[pl.ds(r0, CHUNK), lanes], preferred_element_type=F32)
            mixed = mixed + bs_ref[:, lanes]
            gt_ref[pl.ds(r0, CHUNK), lanes] = (u_ref[pl.ds(r0, CHUNK), lanes].astype(F32)
                                               * mixed).astype(BF16)
        _for_chunks(tm, CHUNK, chunk)

    o_ref[...] = x_ref[...] + jnp.dot(gt_ref[...], wo_ref[...], preferred_element_type=F32)


def _sg_prompt(u, v, ln_g, ln_b, w_s, bs_map, w_o, x, tm):
    m = u.shape[0]
    vspec = pl.BlockSpec((1, D_MODEL), lambda i: (0, 0))
    row = pl.BlockSpec((tm, D_MODEL), lambda i: (i, 0))
    return pl.pallas_call(
        functools.partial(_sg_prompt_kernel, tm=tm),
        grid=(m // tm,),
        in_specs=[row, row, vspec, vspec,
                  pl.BlockSpec((N_SG_HEADS, CHUNK, CHUNK), lambda i: (0, 0, 0)),
                  pl.BlockSpec((CHUNK, D_MODEL), lambda i: (0, 0)),
                  pl.BlockSpec((D_MODEL, D_MODEL), lambda i: (0, 0)),
                  row],
        out_specs=row,
        out_shape=jax.ShapeDtypeStruct((m, D_MODEL), F32),
        scratch_shapes=[pltpu.VMEM((tm, D_MODEL), BF16), pltpu.VMEM((tm, D_MODEL), BF16)],
        compiler_params=_params("parallel"),
    )(u, v, ln_g.reshape(1, D_MODEL), ln_b.reshape(1, D_MODEL), w_s, bs_map, w_o, x)


def _sg_sample_kernel(u_ref, v_ref, lng_ref, lnb_ref, ws0_ref, bs0_ref, wo_ref, x_ref,
                      o_ref, vn_ref):
    vn = _ln_rows(v_ref[...], lng_ref[...], lnb_ref[...])
    vn_ref[...] = vn
    mixed = ws0_ref[...] * vn + bs0_ref[...]
    gated = (u_ref[...].astype(F32) * mixed).astype(BF16)
    o_ref[...] = x_ref[...] + jnp.dot(gated, wo_ref[...], preferred_element_type=F32)


def _sg_sample(u, v, ln_g, ln_b, ws0_map, bs0_map, w_o, x):
    n = u.shape[0]
    vspec = pl.BlockSpec((1, D_MODEL), lambda i: (0, 0))
    row = pl.BlockSpec((n, D_MODEL), lambda i: (0, 0))
    return pl.pallas_call(
        _sg_sample_kernel,
        grid=(1,),
        in_specs=[row, row, vspec, vspec, vspec, vspec,
                  pl.BlockSpec((D_MODEL, D_MODEL), lambda i: (0, 0)), row],
        out_specs=[row, row],
        out_shape=[jax.ShapeDtypeStruct((n, D_MODEL), F32),
                   jax.ShapeDtypeStruct((n, D_MODEL), F32)],
        compiler_params=_params("arbitrary"),
    )(u, v, ln_g.reshape(1, D_MODEL), ln_b.reshape(1, D_MODEL), ws0_map, bs0_map, w_o, x)


def _trunk(x3, is_prompt, st_a, st_b, st_c, p):
    b, l, _ = x3.shape
    m = b * l
    x = x3.reshape(m, D_MODEL)
    tm = 1024 if is_prompt else m
    tm_res = 512 if is_prompt else m
    tt = 256
    flat3 = lambda a: a.reshape(b, l, D_MODEL)

    (g,) = _mm_parts(x, p["norm_mix"][0], p["a_w_pw1"], p["a_b_pw1"], 2, (F32,),
                     _glu_epilogue, tm, 512)
    if is_prompt:
        c = _conv_a_prompt(flat3(g), p["a_w_dw"], p["a_b_dw"], p["a_ln_g"], p["a_ln_b"], tt)
        c = c.reshape(m, D_MODEL)
        new_a = flat3(g)[:, l - (CONV_A_WIDTH - 1):]
    else:
        c = _conv_a_sample(st_a, g, p["a_w_dw"], p["a_b_dw"], p["a_ln_g"], p["a_ln_b"], 32)
        new_a = jnp.concatenate([st_a[:, 1:], g[:, None, :]], axis=1)
    x = _mm_res(c, p["a_w_pw2"], x, tm_res)
    x = _mlp(x, p["norm_mlp"][0], p["mlp_w1"][0], p["mlp_w2"][0], tm, 512)

    if is_prompt:
        x3n, hs = _pool_prompt(flat3(x), p["norm_mix"][1], p["b_w_grp"], p["b_scale"], tt)
        x = x3n.reshape(m, D_MODEL)
        new_b = hs[:, -1, 16 - POOL_STATE:]
    else:
        x, h = _pool_sample(x, st_b, p["norm_mix"][1], p["b_w_grp"], p["b_scale"], 32)
        new_b = jnp.concatenate([st_b[:, 1:], h[:, None, :]], axis=1)
    x = _mlp(x, p["norm_mlp"][1], p["mlp_w1"][1], p["mlp_w2"][1], tm, 512)

    bg, cx = _mm_parts(x, p["norm_mix"][2], p["c_w_in"], None, 3, (F32, F32),
                       _gate3_epilogue, tm, 256)
    if is_prompt:
        z = _conv_c_prompt(flat3(bg), flat3(cx), p["c_w_conv"], tt).reshape(m, D_MODEL)
        new_c = flat3(cx)[:, l - (CONV_C_WIDTH - 1):]
    else:
        z = _conv_c_sample(bg, cx, st_c, p["c_w_conv"])
        new_c = jnp.concatenate([st_c[:, 1:], cx[:, None, :]], axis=1)
    x = _mm_res(z, p["c_w_out"], x, tm_res)
    x = _mlp(x, p["norm_mlp"][2], p["mlp_w1"][2], p["mlp_w2"][2], tm, 512)

    u, v = _mm_parts(x, p["norm_mix"][3], p["d_w_uv"], None, 2, (BF16, F32),
                     _gelu_epilogue, tm, 512)
    if is_prompt:
        bs_map = jnp.repeat(p["d_b_s"].T, SG_HEAD_DIM, axis=1)
        x = _sg_prompt(u, v, p["d_ln_g"], p["d_ln_b"], p["d_w_s"], bs_map, p["d_w_o"], x, 512)
        new_d = None
    else:
        ws0_map = jnp.repeat(p["d_w_s"][:, 0, 0], SG_HEAD_DIM).reshape(1, D_MODEL)
        bs0_map = jnp.repeat(p["d_b_s"][:, 0], SG_HEAD_DIM).reshape(1, D_MODEL)
        x, new_d = _sg_sample(u, v, p["d_ln_g"], p["d_ln_b"], ws0_map, bs0_map, p["d_w_o"], x)
        new_d = new_d.reshape(b, l, D_MODEL)
    y = _mlp(x, p["norm_mlp"][3], p["mlp_w1"][3], p["mlp_w2"][3], tm, 512,
             final_g=p["norm_final"])
    return y.reshape(b, l, D_MODEL), new_a, new_b, new_c, new_d


def kernel(x_prompt, x_sample, state_conv_a, state_pool, state_short_conv, norm_mix, norm_mlp, norm_final, a_w_pw1, a_b_pw1, a_w_dw, a_b_dw, a_ln_g, a_ln_b, a_w_pw2, b_w_grp, b_scale, c_w_in, c_w_conv, c_w_out, d_w_uv, d_ln_g, d_ln_b, d_w_s, d_b_s, d_w_o, mlp_w1, mlp_w2):
    bf = lambda w: w.astype(BF16)
    p = dict(
        norm_mix=norm_mix, norm_mlp=norm_mlp, norm_final=norm_final,
        a_w_pw1=bf(a_w_pw1[0]), a_b_pw1=a_b_pw1[0], a_w_dw=a_w_dw[0], a_b_dw=a_b_dw[0],
        a_ln_g=a_ln_g[0], a_ln_b=a_ln_b[0], a_w_pw2=bf(a_w_pw2[0]),
        b_w_grp=bf(b_w_grp[0]), b_scale=b_scale[0],
        c_w_in=bf(c_w_in[0]), c_w_conv=c_w_conv[0], c_w_out=bf(c_w_out[0]),
        d_w_uv=bf(d_w_uv[0]), d_ln_g=d_ln_g[0], d_ln_b=d_ln_b[0], d_w_s=d_w_s[0],
        d_b_s=d_b_s[0], d_w_o=bf(d_w_o[0]),
        mlp_w1=bf(mlp_w1), mlp_w2=bf(mlp_w2),
    )
    y_p, pa, pb, pc, _ = _trunk(x_prompt, True, None, None, None, p)
    xs = x_sample.reshape(x_sample.shape[0], 1, D_MODEL)
    y_s, sa, sb, sc, sd = _trunk(xs.reshape(1, -1, D_MODEL), False, state_conv_a[0],
                                 state_pool[0], state_short_conv[0], p)
    n = x_sample.shape[0]
    return (y_p, y_s.reshape(n, 1, D_MODEL),
            pa[None], sa[None], pb[None], sb[None], pc[None], sc[None],
            sd.reshape(n, 1, D_MODEL)[None])
```

```python
import functools

import jax
import jax.numpy as jnp
from jax import lax
from jax.experimental import pallas as pl
from jax.experimental.pallas import tpu as pltpu

D_MODEL = 2048
D_FF = 4 * D_MODEL
CONV_A_WIDTH = 31
POOL_WINDOWS = (2, 4, 8, 16)
POOL_GROUP = D_MODEL // len(POOL_WINDOWS)
POOL_STATE = max(POOL_WINDOWS) - 1
CONV_C_WIDTH = 3
CHUNK = 128
N_SG_HEADS = 8
SG_HEAD_DIM = D_MODEL // N_SG_HEADS
PAST_LEN = 16384
RMS_EPS = 1e-6
LN_EPS = 1e-5

SUBLANES = 8
VMEM_LIMIT_BYTES = 56 * 1024 * 1024
HALO = 32

TM = 1024
TT = 256

F32 = jnp.float32
BF16 = jnp.bfloat16


def _params(*sem):
    return pltpu.CompilerParams(dimension_semantics=sem, vmem_limit_bytes=VMEM_LIMIT_BYTES)


def _for_chunks(n, chunk, body):
    assert n % chunk == 0
    if n == chunk:
        body(0)
    else:
        def step(i, c):
            body(pl.multiple_of(i * chunk, chunk))
            return c
        lax.fori_loop(0, n // chunk, step, 0)


def _rms_rows(x, g):
    r = lax.rsqrt(jnp.mean(x * x, axis=-1, keepdims=True) + RMS_EPS)
    return x * r * g


def _ln_rows(x, g, b):
    mu = jnp.mean(x, axis=-1, keepdims=True)
    xc = x - mu
    var = jnp.mean(xc * xc, axis=-1, keepdims=True)
    return xc * lax.rsqrt(var + LN_EPS) * g + b


def _row_chunk(tm):
    for c in (64, 32, 16, 8):
        if tm % c == 0:
            return c
    raise ValueError(tm)


def _rms_to_bf16(x_ref, g_ref, xn_ref, copy_ref=None):
    n = x_ref.shape[0]
    rc = _row_chunk(n)

    def body(r0):
        x = x_ref[pl.ds(r0, rc), :]
        xn_ref[pl.ds(r0, rc), :] = _rms_rows(x, g_ref[...]).astype(BF16)
        if copy_ref is not None:
            copy_ref[pl.ds(r0, rc), :] = x
    _for_chunks(n, rc, body)


def _vec(v):
    return v.reshape(1, D_MODEL)


def _sample_col(i, j, ni):
    return jnp.where(i == ni - 1, j, 0)


def _mm_parts_kernel(*refs, n_parts, n_out, has_bias, epilogue):
    xp_ref, xs_ref, g_ref = refs[:3]
    w_refs = refs[3:3 + n_parts]
    pos = 3 + n_parts
    b_refs = refs[pos:pos + n_parts] if has_bias else ()
    pos += n_parts if has_bias else 0
    op_refs = refs[pos:pos + n_out]
    os_refs = refs[pos + n_out:pos + 2 * n_out]
    xnp_ref, xns_ref = refs[pos + 2 * n_out:]
    i, j = pl.program_id(0), pl.program_id(1)
    last_i = pl.num_programs(0) - 1

    @pl.when(j == 0)
    def _():
        _rms_to_bf16(xp_ref, g_ref, xnp_ref)

    @pl.when((j == 0) & (i == last_i))
    def _():
        _rms_to_bf16(xs_ref, g_ref, xns_ref)

    def rows(xn_ref, o_refs):
        parts = []
        for p in range(n_parts):
            a = jnp.dot(xn_ref[...], w_refs[p][...], preferred_element_type=F32)
            if has_bias:
                a = a + b_refs[p][...]
            parts.append(a)
        for o_ref, o in zip(o_refs, epilogue(*parts)):
            o_ref[...] = o.astype(o_ref.dtype)

    rows(xnp_ref, op_refs)

    @pl.when(i == last_i)
    def _():
        rows(xns_ref, os_refs)


def _mm_parts(name, xp, xs, g, w, layer, bias, n_parts, out_dtypes, epilogue, tn):
    mp, ms = xp.shape[0], xs.shape[0]
    nj = D_MODEL // tn
    ni = mp // TM
    in_specs = [pl.BlockSpec((TM, D_MODEL), lambda i, j: (i, 0)),
                pl.BlockSpec((ms, D_MODEL), lambda i, j: (0, 0)),
                pl.BlockSpec((1, D_MODEL), lambda i, j: (0, 0))]
    args = [xp, xs, _vec(g)]
    for p in range(n_parts):
        in_specs.append(pl.BlockSpec((None, D_MODEL, tn), lambda i, j, p=p: (layer, 0, p * nj + j)))
        args.append(w)
    if bias is not None:
        b3 = bias.reshape(bias.shape[0], 1, n_parts * D_MODEL)
        for p in range(n_parts):
            in_specs.append(pl.BlockSpec((None, 1, tn), lambda i, j, p=p: (layer, 0, p * nj + j)))
            args.append(b3)
    kern = functools.partial(_mm_parts_kernel, n_parts=n_parts, n_out=len(out_dtypes),
                             has_bias=bias is not None, epilogue=epilogue)
    outs = pl.pallas_call(
        kern,
        grid=(ni, nj),
        in_specs=in_specs,
        out_specs=([pl.BlockSpec((TM, tn), lambda i, j: (i, j)) for _ in out_dtypes]
                   + [pl.BlockSpec((ms, tn), lambda i, j: (0, _sample_col(i, j, ni)))
                      for _ in out_dtypes]),
        out_shape=([jax.ShapeDtypeStruct((mp, D_MODEL), dt) for dt in out_dtypes]
                   + [jax.ShapeDtypeStruct((ms, D_MODEL), dt) for dt in out_dtypes]),
        scratch_shapes=[pltpu.VMEM((TM, D_MODEL), BF16), pltpu.VMEM((ms, D_MODEL), BF16)],
        compiler_params=_params("arbitrary", "arbitrary"),
        name=name,
    )(*args)
    n = len(out_dtypes)
    return outs[:n], outs[n:]


def _glu_epilogue(a, b):
    return (a * jax.nn.sigmoid(b),)


def _gelu_epilogue(a, b):
    return (jax.nn.gelu(a), jax.nn.gelu(b))


def _gate3_epilogue(b, c, xv):
    return (b, c * xv)


def _mm_res_kernel(ap_ref, as_ref, w_ref, xp_ref, xs_ref, op_ref, os_ref):
    op_ref[...] = xp_ref[...] + jnp.dot(ap_ref[...], w_ref[...], preferred_element_type=F32)

    @pl.when(pl.program_id(0) == pl.num_programs(0) - 1)
    def _():
        os_ref[...] = xs_ref[...] + jnp.dot(as_ref[...], w_ref[...], preferred_element_type=F32)


def _mm_res(name, ap, as_, w, layer, xp, xs, tn):
    mp, ms = ap.shape[0], as_.shape[0]
    ni = mp // TM
    tile_s = pl.BlockSpec((ms, tn), lambda i, j: (0, _sample_col(i, j, ni)))
    return pl.pallas_call(
        _mm_res_kernel,
        grid=(ni, D_MODEL // tn),
        in_specs=[pl.BlockSpec((TM, D_MODEL), lambda i, j: (i, 0)),
                  pl.BlockSpec((ms, D_MODEL), lambda i, j: (0, 0)),
                  pl.BlockSpec((None, D_MODEL, tn), lambda i, j: (layer, 0, j)),
                  pl.BlockSpec((TM, tn), lambda i, j: (i, j)),
                  tile_s],
        out_specs=[pl.BlockSpec((TM, tn), lambda i, j: (i, j)), tile_s],
        out_shape=[jax.ShapeDtypeStruct((mp, D_MODEL), F32),
                   jax.ShapeDtypeStruct((ms, D_MODEL), F32)],
        compiler_params=_params("arbitrary", "arbitrary"),
        name=name,
    )(ap, as_, w, xp, xs)


def _mlp_kernel(*refs, final):
    if final:
        xp_ref, xs_ref, g_ref, w1_ref, w2_ref, gf_ref = refs[:6]
        refs = refs[6:]
    else:
        xp_ref, xs_ref, g_ref, w1_ref, w2_ref = refs[:5]
        refs = refs[5:]
    op_ref, os_ref, xnp_ref, xns_ref = refs
    i, j = pl.program_id(0), pl.program_id(1)
    last_i = pl.num_programs(0) - 1
    last_j = pl.num_programs(1) - 1

    @pl.when(j == 0)
    def _():
        _rms_to_bf16(xp_ref, g_ref, xnp_ref, copy_ref=op_ref)

    @pl.when((j == 0) & (i == last_i))
    def _():
        _rms_to_bf16(xs_ref, g_ref, xns_ref, copy_ref=os_ref)

    def rows(xn_ref, o_ref):
        h = jnp.dot(xn_ref[...], w1_ref[...], preferred_element_type=F32)
        h = jnp.square(jnp.maximum(h, 0.0)).astype(BF16)
        o_ref[...] += jnp.dot(h, w2_ref[...], preferred_element_type=F32)

    rows(xnp_ref, op_ref)

    @pl.when(i == last_i)
    def _():
        rows(xns_ref, os_ref)

    if final:
        def norm(o_ref):
            n = o_ref.shape[0]
            rc = _row_chunk(n)

            def body(r0):
                o_ref[pl.ds(r0, rc), :] = _rms_rows(o_ref[pl.ds(r0, rc), :], gf_ref[...])
            _for_chunks(n, rc, body)

        @pl.when(j == last_j)
        def _():
            norm(op_ref)

        @pl.when((j == last_j) & (i == last_i))
        def _():
            norm(os_ref)


def _mlp(name, xp, xs, g, w1, w2, layer, tf, final_g=None):
    mp, ms = xp.shape[0], xs.shape[0]
    final = final_g is not None
    in_specs = [pl.BlockSpec((TM, D_MODEL), lambda i, j: (i, 0)),
                pl.BlockSpec((ms, D_MODEL), lambda i, j: (0, 0)),
                pl.BlockSpec((1, D_MODEL), lambda i, j: (0, 0)),
                pl.BlockSpec((None, D_MODEL, tf), lambda i, j: (layer, 0, j)),
                pl.BlockSpec((None, tf, D_MODEL), lambda i, j: (layer, j, 0))]
    args = [xp, xs, _vec(g), w1, w2]
    if final:
        in_specs.append(pl.BlockSpec((1, D_MODEL), lambda i, j: (0, 0)))
        args.append(_vec(final_g))
    return pl.pallas_call(
        functools.partial(_mlp_kernel, final=final),
        grid=(mp // TM, D_FF // tf),
        in_specs=in_specs,
        out_specs=[pl.BlockSpec((TM, D_MODEL), lambda i, j: (i, 0)),
                   pl.BlockSpec((ms, D_MODEL), lambda i, j: (0, 0))],
        out_shape=[jax.ShapeDtypeStruct((mp, D_MODEL), F32),
                   jax.ShapeDtypeStruct((ms, D_MODEL), F32)],
        scratch_shapes=[pltpu.VMEM((TM, D_MODEL), BF16), pltpu.VMEM((ms, D_MODEL), BF16)],
        compiler_params=_params("arbitrary", "arbitrary"),
        name=name,
    )(*args)


CONV_ROWS = 64
CONV_LANES = 256


def _carry_halo(ext_ref, t, tt, halo):
    @pl.when(t == 0)
    def _():
        ext_ref[pl.ds(0, halo), :] = jnp.zeros((halo, D_MODEL), F32)

    @pl.when(t > 0)
    def _():
        ext_ref[pl.ds(0, halo), :] = ext_ref[pl.ds(tt, halo), :]


def _conv_a_kernel(g_ref, w_ref, bdw_ref, lng_ref, lnb_ref, o_ref, ext_ref, c_ref, *, tt):
    _carry_halo(ext_ref, pl.program_id(1), tt, HALO)

    def copy_in(r0):
        ext_ref[pl.ds(HALO + r0, CONV_ROWS), :] = g_ref[0, pl.ds(r0, CONV_ROWS), :]
    _for_chunks(tt, CONV_ROWS, copy_in)

    off = HALO - (CONV_A_WIDTH - 1)
    win = CONV_ROWS + HALO
    for c0 in range(0, D_MODEL, CONV_LANES):
        lanes = pl.ds(c0, CONV_LANES)

        def rows(r0, lanes=lanes):
            w = ext_ref[pl.ds(r0, win), lanes]
            acc = jnp.zeros((CONV_ROWS, CONV_LANES), F32)
            for s in range(SUBLANES):
                ws = w if s == 0 else pltpu.roll(w, win - s, 0)
                for k in range(CONV_A_WIDTH):
                    if (off + k) % SUBLANES == s:
                        a = off + k - s
                        acc = acc + ws[a:a + CONV_ROWS] * w_ref[pl.ds(k, 1), lanes]
            c_ref[pl.ds(r0, CONV_ROWS), lanes] = acc
        _for_chunks(tt, CONV_ROWS, rows)

    rc = 32

    def norm(r0):
        c = c_ref[pl.ds(r0, rc), :] + bdw_ref[...]
        y = _ln_rows(c, lng_ref[...], lnb_ref[...])
        o_ref[0, pl.ds(r0, rc), :] = (y * jax.nn.sigmoid(y)).astype(BF16)
    _for_chunks(tt, rc, norm)


def _pad_rows(w, rows):
    return jnp.zeros((rows, D_MODEL), F32).at[:w.shape[0]].set(w)


def _conv_a_prompt(g3, w_dw, b_dw, ln_g, ln_b):
    b, l, _ = g3.shape
    vspec = pl.BlockSpec((1, D_MODEL), lambda i, t: (0, 0))
    return pl.pallas_call(
        functools.partial(_conv_a_kernel, tt=TT),
        grid=(b, l // TT),
        in_specs=[pl.BlockSpec((1, TT, D_MODEL), lambda i, t: (i, t, 0)),
                  pl.BlockSpec((32, D_MODEL), lambda i, t: (0, 0)),
                  vspec, vspec, vspec],
        out_specs=pl.BlockSpec((1, TT, D_MODEL), lambda i, t: (i, t, 0)),
        out_shape=jax.ShapeDtypeStruct((b, l, D_MODEL), BF16),
        scratch_shapes=[pltpu.VMEM((TT + HALO, D_MODEL), F32),
                        pltpu.VMEM((TT, D_MODEL), F32)],
        compiler_params=_params("arbitrary", "arbitrary"),
        name="conv_a_prompt",
    )(g3, _pad_rows(w_dw, 32), _vec(b_dw), _vec(ln_g), _vec(ln_b))


def _conv_a_sample_kernel(st_ref, g_ref, w_ref, bdw_ref, lng_ref, lnb_ref, o_ref, ns_ref):
    n_st = CONV_A_WIDTH - 1
    acc = g_ref[...] * w_ref[pl.ds(n_st, 1), :]
    for k in range(n_st):
        acc = acc + st_ref[:, k, :] * w_ref[pl.ds(k, 1), :]
    y = _ln_rows(acc + bdw_ref[...], lng_ref[...], lnb_ref[...])
    o_ref[...] = (y * jax.nn.sigmoid(y)).astype(BF16)
    ns_ref[:, pl.ds(0, n_st - 1), :] = st_ref[:, pl.ds(1, n_st - 1), :]
    ns_ref[:, n_st - 1, :] = g_ref[...]


def _conv_a_sample(state, g, w_dw, b_dw, ln_g, ln_b, tb):
    n = g.shape[0]
    n_st = CONV_A_WIDTH - 1
    vspec = pl.BlockSpec((1, D_MODEL), lambda i: (0, 0))
    st_spec = pl.BlockSpec((tb, n_st, D_MODEL), lambda i: (i, 0, 0))
    return pl.pallas_call(
        _conv_a_sample_kernel,
        grid=(n // tb,),
        in_specs=[st_spec,
                  pl.BlockSpec((tb, D_MODEL), lambda i: (i, 0)),
                  pl.BlockSpec((32, D_MODEL), lambda i: (0, 0)),
                  vspec, vspec, vspec],
        out_specs=[pl.BlockSpec((tb, D_MODEL), lambda i: (i, 0)), st_spec],
        out_shape=[jax.ShapeDtypeStruct((n, D_MODEL), BF16),
                   jax.ShapeDtypeStruct((n, n_st, D_MODEL), F32)],
        compiler_params=_params("arbitrary"),
        name="conv_a_sample",
    )(state, g, _pad_rows(w_dw, 32), _vec(b_dw), _vec(ln_g), _vec(ln_b))


POOL_HALO = 16


def _pool_prompt_kernel(x_ref, g_ref, w_ref, sc_ref, o_ref, hs_ref, ext_ref, diff_ref, wb_ref,
                        *, tt):
    t = pl.program_id(1)
    H = POOL_HALO
    _carry_halo(ext_ref, t, tt, H)

    rc = 32

    def norm(r0):
        ext_ref[pl.ds(H + r0, rc), :] = _rms_rows(x_ref[0, pl.ds(r0, rc), :], g_ref[...])
    _for_chunks(tt, rc, norm)

    hs_ref[0, 0] = ext_ref[pl.ds(tt, H), :]

    @pl.when((pl.program_id(0) == 0) & (t == 0))
    def _():
        wb_ref[...] = w_ref[...].astype(BF16)

    pr = 64
    win = pr + H
    for gi, w in enumerate(POOL_WINDOWS):
        lanes = pl.ds(gi * POOL_GROUP, POOL_GROUP)

        def rows(r0, w=w, lanes=lanes):
            s = ext_ref[pl.ds(r0, win), lanes]
            cur = s[H:]
            d = 1
            while d < w:
                s = s + pltpu.roll(s, d, 0)
                d *= 2
            pos = t * tt + r0 + lax.broadcasted_iota(jnp.int32, (pr, 1), 0)
            cnt = jnp.minimum(pos + 1, w).astype(F32)
            diff_ref[pl.ds(r0, pr), lanes] = (s[H:] / cnt - cur).astype(BF16)
        _for_chunks(tt, pr, rows)

    for gi in range(len(POOL_WINDOWS)):
        sl = slice(gi * POOL_GROUP, (gi + 1) * POOL_GROUP)
        y = jnp.dot(diff_ref[:, sl], wb_ref[gi], preferred_element_type=F32)
        o_ref[0, :, sl] = x_ref[0, :, sl] + y * sc_ref[:, sl]


def _pool_prompt(x3, g, w_grp, layer, scale):
    b, l, _ = x3.shape
    nt = l // TT
    ng = len(POOL_WINDOWS)
    vspec = pl.BlockSpec((1, D_MODEL), lambda i, t: (0, 0))
    return pl.pallas_call(
        functools.partial(_pool_prompt_kernel, tt=TT),
        grid=(b, nt),
        in_specs=[pl.BlockSpec((1, TT, D_MODEL), lambda i, t: (i, t, 0)),
                  vspec,
                  pl.BlockSpec((None, ng, POOL_GROUP, POOL_GROUP), lambda i, t: (layer, 0, 0, 0)),
                  vspec],
        out_specs=[pl.BlockSpec((1, TT, D_MODEL), lambda i, t: (i, t, 0)),
                   pl.BlockSpec((1, 1, POOL_HALO, D_MODEL), lambda i, t: (i, t, 0, 0))],
        out_shape=[jax.ShapeDtypeStruct((b, l, D_MODEL), F32),
                   jax.ShapeDtypeStruct((b, nt, POOL_HALO, D_MODEL), F32)],
        scratch_shapes=[pltpu.VMEM((TT + POOL_HALO, D_MODEL), F32),
                        pltpu.VMEM((TT, D_MODEL), BF16),
                        pltpu.VMEM((ng, POOL_GROUP, POOL_GROUP), BF16)],
        compiler_params=_params("arbitrary", "arbitrary"),
        name="pool_prompt",
    )(x3, _vec(g), w_grp, _vec(scale))


def _pool_sample_kernel(x_ref, st_ref, g_ref, w_ref, sc_ref, o_ref, ns_ref):
    x = x_ref[...]
    h = _rms_rows(x, g_ref[...])
    for gi, w in enumerate(POOL_WINDOWS):
        sl = slice(gi * POOL_GROUP, (gi + 1) * POOL_GROUP)
        cur = h[:, sl]
        s = cur
        for d in range(1, w):
            s = s + st_ref[:, POOL_STATE - d, sl]
        diff = s / float(min(PAST_LEN + 1, w)) - cur
        y = jnp.dot(diff.astype(BF16), w_ref[gi].astype(BF16), preferred_element_type=F32)
        o_ref[:, sl] = x[:, sl] + y * sc_ref[:, sl]
    ns_ref[:, pl.ds(0, POOL_STATE - 1), :] = st_ref[:, pl.ds(1, POOL_STATE - 1), :]
    ns_ref[:, POOL_STATE - 1, :] = h


def _pool_sample(x, state, g, w_grp, layer, scale, tb):
    n = x.shape[0]
    ng = len(POOL_WINDOWS)
    vspec = pl.BlockSpec((1, D_MODEL), lambda i: (0, 0))
    row = pl.BlockSpec((tb, D_MODEL), lambda i: (i, 0))
    st_spec = pl.BlockSpec((tb, POOL_STATE, D_MODEL), lambda i: (i, 0, 0))
    return pl.pallas_call(
        _pool_sample_kernel,
        grid=(n // tb,),
        in_specs=[row, st_spec, vspec,
                  pl.BlockSpec((None, ng, POOL_GROUP, POOL_GROUP), lambda i: (layer, 0, 0, 0)),
                  vspec],
        out_specs=[row, st_spec],
        out_shape=[jax.ShapeDtypeStruct((n, D_MODEL), F32),
                   jax.ShapeDtypeStruct((n, POOL_STATE, D_MODEL), F32)],
        compiler_params=_params("arbitrary"),
        name="pool_sample",
    )(x, state, _vec(g), w_grp, _vec(scale))


def _conv_c_prompt_kernel(b_ref, cx_ref, w_ref, o_ref, ext_ref, *, tt):
    H = SUBLANES
    _carry_halo(ext_ref, pl.program_id(1), tt, H)

    rc = 64

    def copy_in(r0):
        ext_ref[pl.ds(H + r0, rc), :] = cx_ref[0, pl.ds(r0, rc), :]
    _for_chunks(tt, rc, copy_in)

    win = rc + H
    cl = 512
    for c0 in range(0, D_MODEL, cl):
        lanes = pl.ds(c0, cl)

        def rows(r0, lanes=lanes):
            w = ext_ref[pl.ds(r0, win), lanes]
            y = (pltpu.roll(w, 2, 0)[H:] * w_ref[pl.ds(0, 1), lanes]
                 + pltpu.roll(w, 1, 0)[H:] * w_ref[pl.ds(1, 1), lanes]
                 + w[H:] * w_ref[pl.ds(2, 1), lanes])
            o_ref[0, pl.ds(r0, rc), lanes] = (b_ref[0, pl.ds(r0, rc), lanes] * y).astype(BF16)
        _for_chunks(tt, rc, rows)


def _conv_c_prompt(b3, cx3, w_conv):
    b, l, _ = b3.shape
    blk = pl.BlockSpec((1, TT, D_MODEL), lambda i, t: (i, t, 0))
    return pl.pallas_call(
        functools.partial(_conv_c_prompt_kernel, tt=TT),
        grid=(b, l // TT),
        in_specs=[blk, blk, pl.BlockSpec((SUBLANES, D_MODEL), lambda i, t: (0, 0))],
        out_specs=blk,
        out_shape=jax.ShapeDtypeStruct((b, l, D_MODEL), BF16),
        scratch_shapes=[pltpu.VMEM((TT + SUBLANES, D_MODEL), F32)],
        compiler_params=_params("arbitrary", "arbitrary"),
        name="conv_c_prompt",
    )(b3, cx3, _pad_rows(w_conv, SUBLANES))


def _conv_c_sample_kernel(b_ref, cx_ref, st_ref, w_ref, o_ref, ns_ref):
    y = (st_ref[:, 0, :] * w_ref[pl.ds(0, 1), :] + st_ref[:, 1, :] * w_ref[pl.ds(1, 1), :]
         + cx_ref[...] * w_ref[pl.ds(2, 1), :])
    o_ref[...] = (b_ref[...] * y).astype(BF16)
    ns_ref[:, 0, :] = st_ref[:, 1, :]
    ns_ref[:, 1, :] = cx_ref[...]


def _conv_c_sample(bg, cx, state, w_conv):
    n = bg.shape[0]
    row = pl.BlockSpec((n, D_MODEL), lambda i: (0, 0))
    st_spec = pl.BlockSpec((n, CONV_C_WIDTH - 1, D_MODEL), lambda i: (0, 0, 0))
    return pl.pallas_call(
        _conv_c_sample_kernel,
        grid=(1,),
        in_specs=[row, row, st_spec, pl.BlockSpec((SUBLANES, D_MODEL), lambda i: (0, 0))],
        out_specs=[row, st_spec],
        out_shape=[jax.ShapeDtypeStruct((n, D_MODEL), BF16),
                   jax.ShapeDtypeStruct((n, CONV_C_WIDTH - 1, D_MODEL), F32)],
        compiler_params=_params("arbitrary"),
        name="conv_c_sample",
    )(bg, cx, state, _pad_rows(w_conv, SUBLANES))


def _sg_kernel(up_ref, vp_ref, us_ref, vs_ref, lng_ref, lnb_ref, ws_ref, bs_ref, ws0_ref, bs0_ref,
               wo_ref, xp_ref, xs_ref, op_ref, os_ref, vns_ref, vn_ref, gtp_ref, gts_ref):
    i, j = pl.program_id(0), pl.program_id(1)
    last_i = pl.num_programs(0) - 1
    tm = up_ref.shape[0]

    @pl.when(j == 0)
    def _():
        rc = 32

        def norm(r0):
            vn_ref[pl.ds(r0, rc), :] = _ln_rows(vp_ref[pl.ds(r0, rc), :], lng_ref[...],
                                                lnb_ref[...]).astype(BF16)
        _for_chunks(tm, rc, norm)

        row = lax.broadcasted_iota(jnp.int32, (CHUNK, CHUNK), 0)
        col = lax.broadcasted_iota(jnp.int32, (CHUNK, CHUNK), 1)
        for hd in range(N_SG_HEADS):
            ws = jnp.where(col <= row, ws_ref[hd], 0.0).astype(BF16)
            lanes = pl.ds(hd * SG_HEAD_DIM, SG_HEAD_DIM)

            def chunk(r0, ws=ws, lanes=lanes):
                mixed = jnp.dot(ws, vn_ref[pl.ds(r0, CHUNK), lanes], preferred_element_type=F32)
                mixed = mixed + bs_ref[:, lanes]
                gtp_ref[pl.ds(r0, CHUNK), lanes] = (up_ref[pl.ds(r0, CHUNK), lanes].astype(F32)
                                                    * mixed).astype(BF16)
            _for_chunks(tm, CHUNK, chunk)

    @pl.when((j == 0) & (i == last_i))
    def _():
        vn = _ln_rows(vs_ref[...], lng_ref[...], lnb_ref[...])
        vns_ref[...] = vn
        mixed = ws0_ref[...] * vn + bs0_ref[...]
        gts_ref[...] = (us_ref[...].astype(F32) * mixed).astype(BF16)

    op_ref[...] = xp_ref[...] + jnp.dot(gtp_ref[...], wo_ref[...], preferred_element_type=F32)

    @pl.when(i == last_i)
    def _():
        os_ref[...] = xs_ref[...] + jnp.dot(gts_ref[...], wo_ref[...], preferred_element_type=F32)


def _sg(up, vp, us, vs, ln_g, ln_b, w_s, layer, bs_map, ws0_map, bs0_map, w_o, xp, xs, tn):
    mp, ms = up.shape[0], us.shape[0]
    ni = mp // TM
    vspec = pl.BlockSpec((1, D_MODEL), lambda i, j: (0, 0))
    rowp = pl.BlockSpec((TM, D_MODEL), lambda i, j: (i, 0))
    rows = pl.BlockSpec((ms, D_MODEL), lambda i, j: (0, 0))
    tilep = pl.BlockSpec((TM, tn), lambda i, j: (i, j))
    tiles = pl.BlockSpec((ms, tn), lambda i, j: (0, _sample_col(i, j, ni)))
    return pl.pallas_call(
        _sg_kernel,
        grid=(ni, D_MODEL // tn),
        in_specs=[rowp, rowp, rows, rows, vspec, vspec,
                  pl.BlockSpec((None, N_SG_HEADS, CHUNK, CHUNK), lambda i, j: (layer, 0, 0, 0)),
                  pl.BlockSpec((CHUNK, D_MODEL), lambda i, j: (0, 0)),
                  vspec, vspec,
                  pl.BlockSpec((None, D_MODEL, tn), lambda i, j: (layer, 0, j)),
                  tilep, tiles],
        out_specs=[tilep, tiles, rows],
        out_shape=[jax.ShapeDtypeStruct((mp, D_MODEL), F32),
                   jax.ShapeDtypeStruct((ms, D_MODEL), F32),
                   jax.ShapeDtypeStruct((ms, D_MODEL), F32)],
        scratch_shapes=[pltpu.VMEM((TM, D_MODEL), BF16), pltpu.VMEM((TM, D_MODEL), BF16),
                        pltpu.VMEM((ms, D_MODEL), BF16)],
        compiler_params=_params("arbitrary", "arbitrary"),
        name="sg_mix",
    )(up, vp, us, vs, _vec(ln_g), _vec(ln_b), w_s, bs_map, ws0_map, bs0_map, w_o, xp, xs)


def kernel(x_prompt, x_sample, state_conv_a, state_pool, state_short_conv, norm_mix, norm_mlp, norm_final, a_w_pw1, a_b_pw1, a_w_dw, a_b_dw, a_ln_g, a_ln_b, a_w_pw2, b_w_grp, b_scale, c_w_in, c_w_conv, c_w_out, d_w_uv, d_ln_g, d_ln_b, d_w_s, d_b_s, d_w_o, mlp_w1, mlp_w2):
    b, l, _ = x_prompt.shape
    mp = b * l
    ms = x_sample.shape[0]
    assert x_sample.shape[1] == 1 and mp % TM == 0 and l % TT == 0
    flat3 = lambda a: a.reshape(b, l, D_MODEL)
    xp = x_prompt.reshape(mp, D_MODEL)
    xs = x_sample.reshape(ms, D_MODEL)

    a_w_pw1, a_w_pw2, c_w_in, c_w_out, d_w_uv, d_w_o, mlp_w1, mlp_w2 = (
        w.astype(BF16) for w in (a_w_pw1, a_w_pw2, c_w_in, c_w_out, d_w_uv, d_w_o, mlp_w1, mlp_w2))

    def mlp(i, xp, xs, final_g=None):
        return _mlp(f"mlp{i}", xp, xs, norm_mlp[i], mlp_w1, mlp_w2, i, 512, final_g=final_g)

    (gp,), (gs,) = _mm_parts("a_pw1_glu", xp, xs, norm_mix[0], a_w_pw1, 0, a_b_pw1, 2, (F32,),
                             _glu_epilogue, 512)
    cp = _conv_a_prompt(flat3(gp), a_w_dw[0], a_b_dw[0], a_ln_g[0], a_ln_b[0]).reshape(mp, D_MODEL)
    cs, sa = _conv_a_sample(state_conv_a[0], gs, a_w_dw[0], a_b_dw[0], a_ln_g[0], a_ln_b[0], 32)
    pa = flat3(gp)[:, l - (CONV_A_WIDTH - 1):]
    xp, xs = _mm_res("a_pw2", cp, cs, a_w_pw2, 0, xp, xs, 512)
    xp, xs = mlp(0, xp, xs)

    xp3, hs = _pool_prompt(flat3(xp), norm_mix[1], b_w_grp, 0, b_scale[0])
    xp = xp3.reshape(mp, D_MODEL)
    pb = hs[:, -1, POOL_HALO - POOL_STATE:]
    xs, sb = _pool_sample(xs, state_pool[0], norm_mix[1], b_w_grp, 0, b_scale[0], 32)
    xp, xs = mlp(1, xp, xs)

    (bgp, cxp), (bgs, cxs) = _mm_parts("c_in_gate", xp, xs, norm_mix[2], c_w_in, 0, None, 3,
                                       (F32, F32), _gate3_epilogue, 256)
    zp = _conv_c_prompt(flat3(bgp), flat3(cxp), c_w_conv[0]).reshape(mp, D_MODEL)
    zs, sc = _conv_c_sample(bgs, cxs, state_short_conv[0], c_w_conv[0])
    pc = flat3(cxp)[:, l - (CONV_C_WIDTH - 1):]
    xp, xs = _mm_res("c_out", zp, zs, c_w_out, 0, xp, xs, 512)
    xp, xs = mlp(2, xp, xs)

    (up, vp), (us, vs) = _mm_parts("d_uv_gelu", xp, xs, norm_mix[3], d_w_uv, 0, None, 2,
                                   (BF16, F32), _gelu_epilogue, 512)
    bs_map = jnp.repeat(d_b_s[0].T, SG_HEAD_DIM, axis=1)
    ws0_map = _vec(jnp.repeat(d_w_s[0, :, 0, 0], SG_HEAD_DIM))
    bs0_map = _vec(jnp.repeat(d_b_s[0, :, 0], SG_HEAD_DIM))
    xp, xs, sd = _sg(up, vp, us, vs, d_ln_g[0], d_ln_b[0], d_w_s, 0, bs_map, ws0_map, bs0_map,
                     d_w_o, xp, xs, 512)
    yp, ys = mlp(3, xp, xs, final_g=norm_final)

    return (yp.reshape(b, l, D_MODEL), ys.reshape(ms, 1, D_MODEL),
            pa[None], sa[None], pb[None], sb[None], pc[None], sc[None],
            sd.reshape(1, ms, 1, D_MODEL))
```

```python
import functools

import jax
import jax.numpy as jnp
from jax import lax
from jax.experimental import pallas as pl
from jax.experimental.pallas import tpu as pltpu

D_MODEL = 2048
D_FF = 4 * D_MODEL
CONV_A_WIDTH = 31
POOL_WINDOWS = (2, 4, 8, 16)
POOL_GROUP = D_MODEL // len(POOL_WINDOWS)
POOL_STATE = max(POOL_WINDOWS) - 1
CONV_C_WIDTH = 3
CHUNK = 128
N_SG_HEADS = 8
SG_HEAD_DIM = D_MODEL // N_SG_HEADS
PAST_LEN = 16384
RMS_EPS = 1e-6
LN_EPS = 1e-5

SUBLANES = 8
BF16_ROWS = 16
VMEM_LIMIT_BYTES = 56 * 1024 * 1024
HALO = 32

TM = 1024
TM_FULL = 512
TT = 256

F32 = jnp.float32
BF16 = jnp.bfloat16


def _params(*sem):
    return pltpu.CompilerParams(dimension_semantics=sem, vmem_limit_bytes=VMEM_LIMIT_BYTES)


def _for_chunks(n, chunk, body, unroll=1):
    assert n % chunk == 0
    if n == chunk:
        body(0)
    else:
        def step(i, c):
            body(pl.multiple_of(i * chunk, chunk))
            return c
        lax.fori_loop(0, n // chunk, step, 0, unroll=unroll)


def _rms_rows(x, g):
    r = lax.rsqrt(jnp.mean(x * x, axis=-1, keepdims=True) + RMS_EPS)
    return x * r * g


def _ln_rows(x, g, b):
    mu = jnp.mean(x, axis=-1, keepdims=True)
    xc = x - mu
    var = jnp.mean(xc * xc, axis=-1, keepdims=True)
    return xc * lax.rsqrt(var + LN_EPS) * g + b


def _row_chunk(tm):
    for c in (64, 32, 16, 8):
        if tm % c == 0:
            return c
    raise ValueError(tm)


def _rms_to_bf16(x_ref, g_ref, xn_ref, copy_ref=None):
    n = x_ref.shape[0]
    rc = _row_chunk(n)

    def body(r0):
        x = x_ref[pl.ds(r0, rc), :]
        xn_ref[pl.ds(r0, rc), :] = _rms_rows(x, g_ref[...]).astype(BF16)
        if copy_ref is not None:
            copy_ref[pl.ds(r0, rc), :] = x
    _for_chunks(n, rc, body, unroll=2)


def _vec(v):
    return v.reshape(1, D_MODEL)


def _sample_col(i, j, ni):
    return jnp.where(i == ni - 1, j, 0)


def _wspec(block, index, layer, **kw):
    if layer is None:
        return pl.BlockSpec(block, index, **kw)
    return pl.BlockSpec((None,) + block, lambda *g: (layer,) + tuple(index(*g)), **kw)


def _cast_job_specs(jobs, n_steps, step_of):
    in_specs, args, out_specs, out_shapes = [], [], [], []
    for src, layer in jobs:
        _, r, c = src.shape
        assert r % (n_steps * BF16_ROWS) == 0, (src.shape, n_steps)
        pr = r // n_steps
        in_specs.append(pl.BlockSpec((None, pr, c),
                                     lambda *g, layer=layer: (layer, step_of(*g), 0)))
        args.append(src)
        out_specs.append(pl.BlockSpec((pr, c), lambda *g: (step_of(*g), 0)))
        out_shapes.append(jax.ShapeDtypeStruct((r, c), BF16))
    return in_specs, args, out_specs, out_shapes


def _run_cast_jobs(src_refs, dst_refs):
    for s_ref, d_ref in zip(src_refs, dst_refs):
        d_ref[...] = s_ref[...].astype(BF16)


def _mm_parts_kernel(*refs, n_parts, n_out, has_bias, n_jobs, epilogue):
    xp_ref, xs_ref, g_ref = refs[:3]
    w_refs = refs[3:3 + n_parts]
    pos = 3 + n_parts
    b_refs = refs[pos:pos + n_parts] if has_bias else ()
    pos += n_parts if has_bias else 0
    job_src = refs[pos:pos + n_jobs]
    pos += n_jobs
    op_refs = refs[pos:pos + n_out]
    os_refs = refs[pos + n_out:pos + 2 * n_out]
    pos += 2 * n_out
    job_dst = refs[pos:pos + n_jobs]
    xnp_ref, xns_ref = refs[pos + n_jobs:]
    i, j = pl.program_id(0), pl.program_id(1)
    last_i = pl.num_programs(0) - 1

    @pl.when(j == 0)
    def _():
        _rms_to_bf16(xp_ref, g_ref, xnp_ref)

    @pl.when((j == 0) & (i == last_i))
    def _():
        _rms_to_bf16(xs_ref, g_ref, xns_ref)

    def rows(xn_ref, o_refs):
        parts = []
        for p in range(n_parts):
            a = jnp.dot(xn_ref[...], w_refs[p][...], preferred_element_type=F32)
            if has_bias:
                a = a + b_refs[p][...]
            parts.append(a)
        for o_ref, o in zip(o_refs, epilogue(*parts)):
            o_ref[...] = o.astype(o_ref.dtype)

    rows(xnp_ref, op_refs)
    _run_cast_jobs(job_src, job_dst)

    @pl.when(i == last_i)
    def _():
        rows(xns_ref, os_refs)


def _mm_parts(name, xp, xs, g, w, layer, bias, n_parts, out_dtypes, epilogue, tn, cast_jobs=()):
    mp, ms = xp.shape[0], xs.shape[0]
    nj = D_MODEL // tn
    ni = mp // TM
    in_specs = [pl.BlockSpec((TM, D_MODEL), lambda i, j: (i, 0)),
                pl.BlockSpec((ms, D_MODEL), lambda i, j: (0, 0)),
                pl.BlockSpec((1, D_MODEL), lambda i, j: (0, 0))]
    args = [xp, xs, _vec(g)]
    for p in range(n_parts):
        in_specs.append(_wspec((D_MODEL, tn), lambda i, j, p=p: (0, p * nj + j), layer))
        args.append(w)
    if bias is not None:
        for p in range(n_parts):
            in_specs.append(pl.BlockSpec((1, tn), lambda i, j, p=p: (0, p * nj + j)))
            args.append(bias.reshape(1, n_parts * D_MODEL))
    j_in, j_args, j_out, j_shapes = _cast_job_specs(cast_jobs, ni * nj, lambda i, j: i * nj + j)
    kern = functools.partial(_mm_parts_kernel, n_parts=n_parts, n_out=len(out_dtypes),
                             has_bias=bias is not None, n_jobs=len(cast_jobs), epilogue=epilogue)
    outs = pl.pallas_call(
        kern,
        grid=(ni, nj),
        in_specs=in_specs + j_in,
        out_specs=([pl.BlockSpec((TM, tn), lambda i, j: (i, j)) for _ in out_dtypes]
                   + [pl.BlockSpec((ms, tn), lambda i, j: (0, _sample_col(i, j, ni)))
                      for _ in out_dtypes] + j_out),
        out_shape=([jax.ShapeDtypeStruct((mp, D_MODEL), dt) for dt in out_dtypes]
                   + [jax.ShapeDtypeStruct((ms, D_MODEL), dt) for dt in out_dtypes] + j_shapes),
        scratch_shapes=[pltpu.VMEM((TM, D_MODEL), BF16), pltpu.VMEM((ms, D_MODEL), BF16)],
        compiler_params=_params("arbitrary", "arbitrary"),
        name=name,
    )(*args, *j_args)
    n = len(out_dtypes)
    return outs[:n], outs[n:2 * n], outs[2 * n:]


def _glu_epilogue(a, b):
    return (a * jax.nn.sigmoid(b),)


def _gelu_epilogue(a, b):
    return (jax.nn.gelu(a), jax.nn.gelu(b))


def _gate3_epilogue(b, c, xv):
    return (b, c * xv)


def _mm_res_kernel(ap_ref, as_ref, w_ref, xp_ref, xs_ref, op_ref, os_ref):
    op_ref[...] = xp_ref[...] + jnp.dot(ap_ref[...], w_ref[...], preferred_element_type=F32)

    @pl.when(pl.program_id(0) == pl.num_programs(0) - 1)
    def _():
        os_ref[...] = xs_ref[...] + jnp.dot(as_ref[...], w_ref[...], preferred_element_type=F32)


def _mm_res(name, ap, as_, w, xp, xs):
    mp, ms = ap.shape[0], as_.shape[0]
    rowp = pl.BlockSpec((TM_FULL, D_MODEL), lambda i: (i, 0))
    rows = pl.BlockSpec((ms, D_MODEL), lambda i: (0, 0))
    return pl.pallas_call(
        _mm_res_kernel,
        grid=(mp // TM_FULL,),
        in_specs=[rowp, rows,
                  pl.BlockSpec((D_MODEL, D_MODEL), lambda i: (0, 0), pipeline_mode=pl.Buffered(1)),
                  rowp, rows],
        out_specs=[rowp, rows],
        out_shape=[jax.ShapeDtypeStruct((mp, D_MODEL), F32),
                   jax.ShapeDtypeStruct((ms, D_MODEL), F32)],
        compiler_params=_params("arbitrary"),
        name=name,
    )(ap, as_, w, xp, xs)


def _mlp_kernel(*refs, final, n_jobs):
    n_in = 6 if final else 5
    xp_ref, xs_ref, g_ref, w1_ref, w2_ref = refs[:5]
    gf_ref = refs[5] if final else None
    job_src = refs[n_in:n_in + n_jobs]
    op_ref, os_ref = refs[n_in + n_jobs:n_in + n_jobs + 2]
    job_dst = refs[n_in + n_jobs + 2:n_in + 2 * n_jobs + 2]
    xnp_ref, xns_ref = refs[n_in + 2 * n_jobs + 2:]
    i, j = pl.program_id(0), pl.program_id(1)
    last_i = pl.num_programs(0) - 1
    last_j = pl.num_programs(1) - 1

    @pl.when(j == 0)
    def _():
        _rms_to_bf16(xp_ref, g_ref, xnp_ref, copy_ref=op_ref)

    @pl.when((j == 0) & (i == last_i))
    def _():
        _rms_to_bf16(xs_ref, g_ref, xns_ref, copy_ref=os_ref)

    def rows(xn_ref, o_ref):
        h = jnp.dot(xn_ref[...], w1_ref[...], preferred_element_type=F32)
        h = jnp.square(jnp.maximum(h, 0.0)).astype(BF16)
        o_ref[...] += jnp.dot(h, w2_ref[...], preferred_element_type=F32)

    rows(xnp_ref, op_ref)
    _run_cast_jobs(job_src, job_dst)

    @pl.when(i == last_i)
    def _():
        rows(xns_ref, os_ref)

    if final:
        def norm(o_ref):
            n = o_ref.shape[0]
            rc = _row_chunk(n)

            def body(r0):
                o_ref[pl.ds(r0, rc), :] = _rms_rows(o_ref[pl.ds(r0, rc), :], gf_ref[...])
            _for_chunks(n, rc, body, unroll=2)

        @pl.when(j == last_j)
        def _():
            norm(op_ref)

        @pl.when((j == last_j) & (i == last_i))
        def _():
            norm(os_ref)


def _mlp(name, xp, xs, g, w1, w2, layer, tf, final_g=None, cast_jobs=()):
    mp, ms = xp.shape[0], xs.shape[0]
    final = final_g is not None
    ni, nj = mp // TM, D_FF // tf
    in_specs = [pl.BlockSpec((TM, D_MODEL), lambda i, j: (i, 0)),
                pl.BlockSpec((ms, D_MODEL), lambda i, j: (0, 0)),
                pl.BlockSpec((1, D_MODEL), lambda i, j: (0, 0)),
                _wspec((D_MODEL, tf), lambda i, j: (0, j), layer),
                _wspec((tf, D_MODEL), lambda i, j: (j, 0), layer)]
    args = [xp, xs, _vec(g), w1, w2]
    if final:
        in_specs.append(pl.BlockSpec((1, D_MODEL), lambda i, j: (0, 0)))
        args.append(_vec(final_g))
    j_in, j_args, j_out, j_shapes = _cast_job_specs(cast_jobs, ni * nj, lambda i, j: i * nj + j)
    outs = pl.pallas_call(
        functools.partial(_mlp_kernel, final=final, n_jobs=len(cast_jobs)),
        grid=(ni, nj),
        in_specs=in_specs + j_in,
        out_specs=[pl.BlockSpec((TM, D_MODEL), lambda i, j: (i, 0)),
                   pl.BlockSpec((ms, D_MODEL), lambda i, j: (0, 0))] + j_out,
        out_shape=[jax.ShapeDtypeStruct((mp, D_MODEL), F32),
                   jax.ShapeDtypeStruct((ms, D_MODEL), F32)] + j_shapes,
        scratch_shapes=[pltpu.VMEM((TM, D_MODEL), BF16), pltpu.VMEM((ms, D_MODEL), BF16)],
        compiler_params=_params("arbitrary", "arbitrary"),
        name=name,
    )(*args, *j_args)
    return outs[0], outs[1], outs[2:]


CONV_ROWS = 128
CONV_LANES = 256


def _carry_halo(ext_ref, t, tt, halo):
    @pl.when(t == 0)
    def _():
        ext_ref[pl.ds(0, halo), :] = jnp.zeros((halo, D_MODEL), F32)

    @pl.when(t > 0)
    def _():
        ext_ref[pl.ds(0, halo), :] = ext_ref[pl.ds(tt, halo), :]


def _conv_a_kernel(g_ref, w_ref, bdw_ref, lng_ref, lnb_ref, o_ref, ext_ref, c_ref, *, tt):
    _carry_halo(ext_ref, pl.program_id(1), tt, HALO)

    def copy_in(r0):
        ext_ref[pl.ds(HALO + r0, 64), :] = g_ref[0, pl.ds(r0, 64), :]
    _for_chunks(tt, 64, copy_in)

    off = HALO - (CONV_A_WIDTH - 1)
    win = CONV_ROWS + HALO
    tiles = CONV_ROWS // SUBLANES
    for c0 in range(0, D_MODEL, CONV_LANES):
        lanes = pl.ds(c0, CONV_LANES)

        def rows(r0, lanes=lanes):
            w = ext_ref[pl.ds(r0, win), lanes]
            acc = jnp.zeros((tiles, SUBLANES, CONV_LANES), F32)
            for s in range(SUBLANES):
                ws = w if s == 0 else pltpu.roll(w, win - s, 0)
                for k in range(CONV_A_WIDTH):
                    if (off + k) % SUBLANES == s:
                        a = off + k - s
                        tap = ws[a:a + CONV_ROWS].reshape(tiles, SUBLANES, CONV_LANES)
                        acc = acc + tap * w_ref[k, :, lanes][None]
            c_ref[pl.ds(r0, CONV_ROWS), lanes] = acc.reshape(CONV_ROWS, CONV_LANES)
        _for_chunks(tt, CONV_ROWS, rows)

    rc = 32

    def norm(r0):
        c = c_ref[pl.ds(r0, rc), :] + bdw_ref[...]
        y = _ln_rows(c, lng_ref[...], lnb_ref[...])
        o_ref[0, pl.ds(r0, rc), :] = (y * jax.nn.sigmoid(y)).astype(BF16)
    _for_chunks(tt, rc, norm, unroll=2)


def _pad_rows(w, rows):
    return jnp.zeros((rows, D_MODEL), F32).at[:w.shape[0]].set(w)


def _conv_a_prompt(g3, w_dw, b_dw, ln_g, ln_b):
    b, l, _ = g3.shape
    vspec = pl.BlockSpec((1, D_MODEL), lambda i, t: (0, 0))
    w_rep = jnp.broadcast_to(_pad_rows(w_dw, 32)[:, None, :], (32, SUBLANES, D_MODEL))
    return pl.pallas_call(
        functools.partial(_conv_a_kernel, tt=TT),
        grid=(b, l // TT),
        in_specs=[pl.BlockSpec((1, TT, D_MODEL), lambda i, t: (i, t, 0)),
                  pl.BlockSpec((32, SUBLANES, D_MODEL), lambda i, t: (0, 0, 0)),
                  vspec, vspec, vspec],
        out_specs=pl.BlockSpec((1, TT, D_MODEL), lambda i, t: (i, t, 0)),
        out_shape=jax.ShapeDtypeStruct((b, l, D_MODEL), BF16),
        scratch_shapes=[pltpu.VMEM((TT + HALO, D_MODEL), F32),
                        pltpu.VMEM((TT, D_MODEL), F32)],
        compiler_params=_params("arbitrary", "arbitrary"),
        name="conv_a_prompt",
    )(g3, w_rep, _vec(b_dw), _vec(ln_g), _vec(ln_b))


def _conv_a_sample_kernel(st_ref, g_ref, w_ref, bdw_ref, lng_ref, lnb_ref, o_ref, ns_ref):
    n_st = CONV_A_WIDTH - 1
    acc = g_ref[...] * w_ref[pl.ds(n_st, 1), :]
    for k in range(n_st):
        acc = acc + st_ref[:, k, :] * w_ref[pl.ds(k, 1), :]
    y = _ln_rows(acc + bdw_ref[...], lng_ref[...], lnb_ref[...])
    o_ref[...] = (y * jax.nn.sigmoid(y)).astype(BF16)
    ns_ref[:, pl.ds(0, n_st - 1), :] = st_ref[:, pl.ds(1, n_st - 1), :]
    ns_ref[:, n_st - 1, :] = g_ref[...]


def _conv_a_sample(state, g, w_dw, b_dw, ln_g, ln_b, tb):
    n = g.shape[0]
    n_st = CONV_A_WIDTH - 1
    vspec = pl.BlockSpec((1, D_MODEL), lambda i: (0, 0))
    st_spec = pl.BlockSpec((tb, n_st, D_MODEL), lambda i: (i, 0, 0))
    return pl.pallas_call(
        _conv_a_sample_kernel,
        grid=(n // tb,),
        in_specs=[st_spec,
                  pl.BlockSpec((tb, D_MODEL), lambda i: (i, 0)),
                  pl.BlockSpec((32, D_MODEL), lambda i: (0, 0)),
                  vspec, vspec, vspec],
        out_specs=[pl.BlockSpec((tb, D_MODEL), lambda i: (i, 0)), st_spec],
        out_shape=[jax.ShapeDtypeStruct((n, D_MODEL), BF16),
                   jax.ShapeDtypeStruct((n, n_st, D_MODEL), F32)],
        compiler_params=_params("arbitrary"),
        name="conv_a_sample",
    )(state, g, _pad_rows(w_dw, 32), _vec(b_dw), _vec(ln_g), _vec(ln_b))


POOL_HALO = 16


def _pool_prompt_kernel(x_ref, g_ref, w_ref, sc_ref, o_ref, hs_ref, ext_ref, diff_ref, wb_ref,
                        *, tt):
    t = pl.program_id(1)
    H = POOL_HALO
    _carry_halo(ext_ref, t, tt, H)

    rc = 32

    def norm(r0):
        ext_ref[pl.ds(H + r0, rc), :] = _rms_rows(x_ref[0, pl.ds(r0, rc), :], g_ref[...])
    _for_chunks(tt, rc, norm, unroll=2)

    hs_ref[0, 0] = ext_ref[pl.ds(tt, H), :]

    @pl.when((pl.program_id(0) == 0) & (t == 0))
    def _():
        wb_ref[...] = w_ref[...].astype(BF16)

    pr = 64
    win = pr + H
    for gi, w in enumerate(POOL_WINDOWS):
        lanes = pl.ds(gi * POOL_GROUP, POOL_GROUP)

        def rows(r0, w=w, lanes=lanes):
            s = ext_ref[pl.ds(r0, win), lanes]
            cur = s[H:]
            d = 1
            while d < w:
                s = s + pltpu.roll(s, d, 0)
                d *= 2
            pos = t * tt + r0 + lax.broadcasted_iota(jnp.int32, (pr, 1), 0)
            cnt = jnp.minimum(pos + 1, w).astype(F32)
            diff_ref[pl.ds(r0, pr), lanes] = (s[H:] / cnt - cur).astype(BF16)
        _for_chunks(tt, pr, rows)

    for gi in range(len(POOL_WINDOWS)):
        sl = slice(gi * POOL_GROUP, (gi + 1) * POOL_GROUP)
        y = jnp.dot(diff_ref[:, sl], wb_ref[gi], preferred_element_type=F32)
        o_ref[0, :, sl] = x_ref[0, :, sl] + y * sc_ref[:, sl]


def _pool_prompt(x3, g, w_grp, layer, scale):
    b, l, _ = x3.shape
    nt = l // TT
    ng = len(POOL_WINDOWS)
    vspec = pl.BlockSpec((1, D_MODEL), lambda i, t: (0, 0))
    return pl.pallas_call(
        functools.partial(_pool_prompt_kernel, tt=TT),
        grid=(b, nt),
        in_specs=[pl.BlockSpec((1, TT, D_MODEL), lambda i, t: (i, t, 0)),
                  vspec,
                  pl.BlockSpec((None, ng, POOL_GROUP, POOL_GROUP), lambda i, t: (layer, 0, 0, 0)),
                  vspec],
        out_specs=[pl.BlockSpec((1, TT, D_MODEL), lambda i, t: (i, t, 0)),
                   pl.BlockSpec((1, 1, POOL_HALO, D_MODEL), lambda i, t: (i, t, 0, 0))],
        out_shape=[jax.ShapeDtypeStruct((b, l, D_MODEL), F32),
                   jax.ShapeDtypeStruct((b, nt, POOL_HALO, D_MODEL), F32)],
        scratch_shapes=[pltpu.VMEM((TT + POOL_HALO, D_MODEL), F32),
                        pltpu.VMEM((TT, D_MODEL), BF16),
                        pltpu.VMEM((ng, POOL_GROUP, POOL_GROUP), BF16)],
        compiler_params=_params("arbitrary", "arbitrary"),
        name="pool_prompt",
    )(x3, _vec(g), w_grp, _vec(scale))


def _pool_sample_kernel(x_ref, st_ref, g_ref, w_ref, sc_ref, o_ref, ns_ref):
    x = x_ref[...]
    h = _rms_rows(x, g_ref[...])
    for gi, w in enumerate(POOL_WINDOWS):
        sl = slice(gi * POOL_GROUP, (gi + 1) * POOL_GROUP)
        cur = h[:, sl]
        s = cur
        for d in range(1, w):
            s = s + st_ref[:, POOL_STATE - d, sl]
        diff = s / float(min(PAST_LEN + 1, w)) - cur
        y = jnp.dot(diff.astype(BF16), w_ref[gi].astype(BF16), preferred_element_type=F32)
        o_ref[:, sl] = x[:, sl] + y * sc_ref[:, sl]
    ns_ref[:, pl.ds(0, POOL_STATE - 1), :] = st_ref[:, pl.ds(1, POOL_STATE - 1), :]
    ns_ref[:, POOL_STATE - 1, :] = h


def _pool_sample(x, state, g, w_grp, layer, scale, tb):
    n = x.shape[0]
    ng = len(POOL_WINDOWS)
    vspec = pl.BlockSpec((1, D_MODEL), lambda i: (0, 0))
    row = pl.BlockSpec((tb, D_MODEL), lambda i: (i, 0))
    st_spec = pl.BlockSpec((tb, POOL_STATE, D_MODEL), lambda i: (i, 0, 0))
    return pl.pallas_call(
        _pool_sample_kernel,
        grid=(n // tb,),
        in_specs=[row, st_spec, vspec,
                  pl.BlockSpec((None, ng, POOL_GROUP, POOL_GROUP), lambda i: (layer, 0, 0, 0)),
                  vspec],
        out_specs=[row, st_spec],
        out_shape=[jax.ShapeDtypeStruct((n, D_MODEL), F32),
                   jax.ShapeDtypeStruct((n, POOL_STATE, D_MODEL), F32)],
        compiler_params=_params("arbitrary"),
        name="pool_sample",
    )(x, state, _vec(g), w_grp, _vec(scale))


def _conv_c_prompt_kernel(b_ref, cx_ref, w_ref, o_ref, ext_ref, *, tt):
    H = SUBLANES
    _carry_halo(ext_ref, pl.program_id(1), tt, H)

    rc = 64

    def copy_in(r0):
        ext_ref[pl.ds(H + r0, rc), :] = cx_ref[0, pl.ds(r0, rc), :]
    _for_chunks(tt, rc, copy_in)

    win = rc + H
    cl = 512
    for c0 in range(0, D_MODEL, cl):
        lanes = pl.ds(c0, cl)

        def rows(r0, lanes=lanes):
            w = ext_ref[pl.ds(r0, win), lanes]
            y = (pltpu.roll(w, 2, 0)[H:] * w_ref[pl.ds(0, 1), lanes]
                 + pltpu.roll(w, 1, 0)[H:] * w_ref[pl.ds(1, 1), lanes]
                 + w[H:] * w_ref[pl.ds(2, 1), lanes])
            o_ref[0, pl.ds(r0, rc), lanes] = (b_ref[0, pl.ds(r0, rc), lanes] * y).astype(BF16)
        _for_chunks(tt, rc, rows)


def _conv_c_prompt(b3, cx3, w_conv):
    b, l, _ = b3.shape
    blk = pl.BlockSpec((1, TT, D_MODEL), lambda i, t: (i, t, 0))
    return pl.pallas_call(
        functools.partial(_conv_c_prompt_kernel, tt=TT),
        grid=(b, l // TT),
        in_specs=[blk, blk, pl.BlockSpec((SUBLANES, D_MODEL), lambda i, t: (0, 0))],
        out_specs=blk,
        out_shape=jax.ShapeDtypeStruct((b, l, D_MODEL), BF16),
        scratch_shapes=[pltpu.VMEM((TT + SUBLANES, D_MODEL), F32)],
        compiler_params=_params("arbitrary", "arbitrary"),
        name="conv_c_prompt",
    )(b3, cx3, _pad_rows(w_conv, SUBLANES))


def _conv_c_sample_kernel(b_ref, cx_ref, st_ref, w_ref, o_ref, ns_ref):
    y = (st_ref[:, 0, :] * w_ref[pl.ds(0, 1), :] + st_ref[:, 1, :] * w_ref[pl.ds(1, 1), :]
         + cx_ref[...] * w_ref[pl.ds(2, 1), :])
    o_ref[...] = (b_ref[...] * y).astype(BF16)
    ns_ref[:, 0, :] = st_ref[:, 1, :]
    ns_ref[:, 1, :] = cx_ref[...]


def _conv_c_sample(bg, cx, state, w_conv):
    n = bg.shape[0]
    row = pl.BlockSpec((n, D_MODEL), lambda i: (0, 0))
    st_spec = pl.BlockSpec((n, CONV_C_WIDTH - 1, D_MODEL), lambda i: (0, 0, 0))
    return pl.pallas_call(
        _conv_c_sample_kernel,
        grid=(1,),
        in_specs=[row, row, st_spec, pl.BlockSpec((SUBLANES, D_MODEL), lambda i: (0, 0))],
        out_specs=[row, st_spec],
        out_shape=[jax.ShapeDtypeStruct((n, D_MODEL), BF16),
                   jax.ShapeDtypeStruct((n, CONV_C_WIDTH - 1, D_MODEL), F32)],
        compiler_params=_params("arbitrary"),
        name="conv_c_sample",
    )(bg, cx, state, _pad_rows(w_conv, SUBLANES))


def _sg_kernel(up_ref, vp_ref, us_ref, vs_ref, lng_ref, lnb_ref, ws_ref, bs_ref, ws0_ref, bs0_ref,
               wo_ref, xp_ref, xs_ref, op_ref, os_ref, vns_ref, vn_ref, gt_ref, wsm_ref):
    i = pl.program_id(0)
    tm = up_ref.shape[0]

    @pl.when(i == 0)
    def _():
        row = lax.broadcasted_iota(jnp.int32, (CHUNK, CHUNK), 0)
        col = lax.broadcasted_iota(jnp.int32, (CHUNK, CHUNK), 1)
        for hd in range(N_SG_HEADS):
            wsm_ref[hd] = jnp.where(col <= row, ws_ref[hd], 0.0).astype(BF16)

    rc = 32

    def norm(r0):
        vn_ref[pl.ds(r0, rc), :] = _ln_rows(vp_ref[pl.ds(r0, rc), :], lng_ref[...],
                                            lnb_ref[...]).astype(BF16)
    _for_chunks(tm, rc, norm, unroll=2)

    for r0 in range(0, tm, CHUNK):
        for hd in range(N_SG_HEADS):
            lanes = pl.ds(hd * SG_HEAD_DIM, SG_HEAD_DIM)
            mixed = jnp.dot(wsm_ref[hd], vn_ref[pl.ds(r0, CHUNK), lanes],
                            preferred_element_type=F32) + bs_ref[:, lanes]
            gt_ref[pl.ds(r0, CHUNK), lanes] = (up_ref[pl.ds(r0, CHUNK), lanes].astype(F32)
                                               * mixed).astype(BF16)

    op_ref[...] = xp_ref[...] + jnp.dot(gt_ref[...], wo_ref[...], preferred_element_type=F32)

    @pl.when(i == pl.num_programs(0) - 1)
    def _():
        vn = _ln_rows(vs_ref[...], lng_ref[...], lnb_ref[...])
        vns_ref[...] = vn
        mixed = ws0_ref[...] * vn + bs0_ref[...]
        gated = (us_ref[...].astype(F32) * mixed).astype(BF16)
        os_ref[...] = xs_ref[...] + jnp.dot(gated, wo_ref[...], preferred_element_type=F32)


def _sg(up, vp, us, vs, ln_g, ln_b, w_s, layer, bs_map, ws0_map, bs0_map, w_o, xp, xs):
    mp, ms = up.shape[0], us.shape[0]
    vspec = pl.BlockSpec((1, D_MODEL), lambda i: (0, 0))
    rowp = pl.BlockSpec((TM_FULL, D_MODEL), lambda i: (i, 0))
    rows = pl.BlockSpec((ms, D_MODEL), lambda i: (0, 0))
    return pl.pallas_call(
        _sg_kernel,
        grid=(mp // TM_FULL,),
        in_specs=[rowp, rowp, rows, rows, vspec, vspec,
                  pl.BlockSpec((None, N_SG_HEADS, CHUNK, CHUNK), lambda i: (layer, 0, 0, 0)),
                  pl.BlockSpec((CHUNK, D_MODEL), lambda i: (0, 0)),
                  vspec, vspec,
                  pl.BlockSpec((D_MODEL, D_MODEL), lambda i: (0, 0), pipeline_mode=pl.Buffered(1)),
                  rowp, rows],
        out_specs=[rowp, rows, rows],
        out_shape=[jax.ShapeDtypeStruct((mp, D_MODEL), F32),
                   jax.ShapeDtypeStruct((ms, D_MODEL), F32),
                   jax.ShapeDtypeStruct((ms, D_MODEL), F32)],
        scratch_shapes=[pltpu.VMEM((TM_FULL, D_MODEL), BF16), pltpu.VMEM((TM_FULL, D_MODEL), BF16),
                        pltpu.VMEM((N_SG_HEADS, CHUNK, CHUNK), BF16)],
        compiler_params=_params("arbitrary"),
        name="sg_mix",
    )(up, vp, us, vs, _vec(ln_g), _vec(ln_b), w_s, bs_map, ws0_map, bs0_map, w_o, xp, xs)


def kernel(x_prompt, x_sample, state_conv_a, state_pool, state_short_conv, norm_mix, norm_mlp, norm_final, a_w_pw1, a_b_pw1, a_w_dw, a_b_dw, a_ln_g, a_ln_b, a_w_pw2, b_w_grp, b_scale, c_w_in, c_w_conv, c_w_out, d_w_uv, d_ln_g, d_ln_b, d_w_s, d_b_s, d_w_o, mlp_w1, mlp_w2):
    b, l, _ = x_prompt.shape
    mp = b * l
    ms = x_sample.shape[0]
    assert x_sample.shape[1] == 1 and mp % TM == 0 and l % TT == 0
    flat3 = lambda a: a.reshape(b, l, D_MODEL)
    xp = x_prompt.reshape(mp, D_MODEL)
    xs = x_sample.reshape(ms, D_MODEL)
    tf = 512

    (gp,), (gs,), (w_pw2, w1_0, w2_0) = _mm_parts(
        "a_pw1_glu", xp, xs, norm_mix[0], a_w_pw1.astype(BF16), 0, a_b_pw1[0], 2, (F32,),
        _glu_epilogue, 512, cast_jobs=[(a_w_pw2, 0), (mlp_w1, 0), (mlp_w2, 0)])
    cp = _conv_a_prompt(flat3(gp), a_w_dw[0], a_b_dw[0], a_ln_g[0], a_ln_b[0]).reshape(mp, D_MODEL)
    cs, sa = _conv_a_sample(state_conv_a[0], gs, a_w_dw[0], a_b_dw[0], a_ln_g[0], a_ln_b[0], 32)
    pa = flat3(gp)[:, l - (CONV_A_WIDTH - 1):]
    xp, xs = _mm_res("a_pw2", cp, cs, w_pw2, xp, xs)
    xp, xs, (w1_1, w2_1, w_in, w_out) = _mlp(
        "mlp0", xp, xs, norm_mlp[0], w1_0, w2_0, None, tf,
        cast_jobs=[(mlp_w1, 1), (mlp_w2, 1), (c_w_in, 0), (c_w_out, 0)])

    xp3, hs = _pool_prompt(flat3(xp), norm_mix[1], b_w_grp, 0, b_scale[0])
    xp = xp3.reshape(mp, D_MODEL)
    pb = hs[:, -1, POOL_HALO - POOL_STATE:]
    xs, sb = _pool_sample(xs, state_pool[0], norm_mix[1], b_w_grp, 0, b_scale[0], 32)
    xp, xs, (w1_2, w2_2, w_uv, w_o) = _mlp(
        "mlp1", xp, xs, norm_mlp[1], w1_1, w2_1, None, tf,
        cast_jobs=[(mlp_w1, 2), (mlp_w2, 2), (d_w_uv, 0), (d_w_o, 0)])

    (bgp, cxp), (bgs, cxs), _ = _mm_parts("c_in_gate", xp, xs, norm_mix[2], w_in, None, None, 3,
                                          (F32, F32), _gate3_epilogue, 256)
    zp = _conv_c_prompt(flat3(bgp), flat3(cxp), c_w_conv[0]).reshape(mp, D_MODEL)
    zs, sc = _conv_c_sample(bgs, cxs, state_short_conv[0], c_w_conv[0])
    pc = flat3(cxp)[:, l - (CONV_C_WIDTH - 1):]
    xp, xs = _mm_res("c_out", zp, zs, w_out, xp, xs)
    xp, xs, (w1_3, w2_3) = _mlp("mlp2", xp, xs, norm_mlp[2], w1_2, w2_2, None, tf,
                                cast_jobs=[(mlp_w1, 3), (mlp_w2, 3)])

    (up, vp), (us, vs), _ = _mm_parts("d_uv_gelu", xp, xs, norm_mix[3], w_uv, None, None, 2,
                                      (BF16, F32), _gelu_epilogue, 512)
    bs_map = jnp.repeat(d_b_s[0].T, SG_HEAD_DIM, axis=1)
    ws0_map = _vec(jnp.repeat(d_w_s[0, :, 0, 0], SG_HEAD_DIM))
    bs0_map = _vec(jnp.repeat(d_b_s[0, :, 0], SG_HEAD_DIM))
    xp, xs, sd = _sg(up, vp, us, vs, d_ln_g[0], d_ln_b[0], d_w_s, 0, bs_map, ws0_map, bs0_map,
                     w_o, xp, xs)
    yp, ys, _ = _mlp("mlp3", xp, xs, norm_mlp[3], w1_3, w2_3, None, tf, final_g=norm_final)

    return (yp.reshape(b, l, D_MODEL), ys.reshape(ms, 1, D_MODEL),
            pa[None], sa[None], pb[None], sb[None], pc[None], sc[None],
            sd.reshape(1, ms, 1, D_MODEL))
```

```python
import functools

import jax
import jax.numpy as jnp
from jax import lax
from jax.experimental import pallas as pl
from jax.experimental.pallas import tpu as pltpu

D_MODEL = 2048
D_FF = 4 * D_MODEL
CONV_A_WIDTH = 31
POOL_WINDOWS = (2, 4, 8, 16)
POOL_GROUP = D_MODEL // len(POOL_WINDOWS)
POOL_STATE = max(POOL_WINDOWS) - 1
CONV_C_WIDTH = 3
CHUNK = 128
N_SG_HEADS = 8
SG_HEAD_DIM = D_MODEL // N_SG_HEADS
PAST_LEN = 16384
RMS_EPS = 1e-6
LN_EPS = 1e-5

SUBLANES = 8
BF16_ROWS = 16
VMEM_LIMIT_BYTES = 56 * 1024 * 1024
HALO = 32

TM = 1024
TM_FULL = 512
TT = 256

F32 = jnp.float32
BF16 = jnp.bfloat16


def _params(*sem):
    return pltpu.CompilerParams(dimension_semantics=sem, vmem_limit_bytes=VMEM_LIMIT_BYTES)


def _for_chunks(n, chunk, body, unroll=1):
    assert n % chunk == 0
    if n == chunk:
        body(0)
    else:
        def step(i, c):
            body(pl.multiple_of(i * chunk, chunk))
            return c
        lax.fori_loop(0, n // chunk, step, 0, unroll=unroll)


def _rms_rows(x, g):
    r = lax.rsqrt(jnp.mean(x * x, axis=-1, keepdims=True) + RMS_EPS)
    return x * r * g


def _ln_rows(x, g, b):
    mu = jnp.mean(x, axis=-1, keepdims=True)
    xc = x - mu
    var = jnp.mean(xc * xc, axis=-1, keepdims=True)
    return xc * lax.rsqrt(var + LN_EPS) * g + b


def _row_chunk(tm):
    for c in (64, 32, 16, 8):
        if tm % c == 0:
            return c
    raise ValueError(tm)


def _rms_to_bf16(x_ref, g_ref, xn_ref, copy_ref=None):
    n = x_ref.shape[0]
    rc = _row_chunk(n)

    def body(r0):
        x = x_ref[pl.ds(r0, rc), :]
        xn_ref[pl.ds(r0, rc), :] = _rms_rows(x, g_ref[...]).astype(BF16)
        if copy_ref is not None:
            copy_ref[pl.ds(r0, rc), :] = x
    _for_chunks(n, rc, body, unroll=2)


def _vec(v):
    return v.reshape(1, D_MODEL)


def _sample_col(i, j, ni):
    return jnp.where(i == ni - 1, j, 0)


def _wspec(block, index, layer, **kw):
    if layer is None:
        return pl.BlockSpec(block, index, **kw)
    return pl.BlockSpec((None,) + block, lambda *g: (layer,) + tuple(index(*g)), **kw)


def _cast_job_specs(jobs, n_steps, step_of):
    in_specs, args, out_specs, out_shapes = [], [], [], []
    for src, layer in jobs:
        _, r, c = src.shape
        assert r % (n_steps * BF16_ROWS) == 0, (src.shape, n_steps)
        pr = r // n_steps
        in_specs.append(pl.BlockSpec((None, pr, c),
                                     lambda *g, layer=layer: (layer, step_of(*g), 0)))
        args.append(src)
        out_specs.append(pl.BlockSpec((pr, c), lambda *g: (step_of(*g), 0)))
        out_shapes.append(jax.ShapeDtypeStruct((r, c), BF16))
    return in_specs, args, out_specs, out_shapes


def _run_cast_jobs(src_refs, dst_refs):
    for s_ref, d_ref in zip(src_refs, dst_refs):
        d_ref[...] = s_ref[...].astype(BF16)


def _mm_parts_kernel(*refs, n_parts, n_out, has_bias, n_jobs, epilogue):
    xp_ref, xs_ref, g_ref = refs[:3]
    w_refs = refs[3:3 + n_parts]
    pos = 3 + n_parts
    b_refs = refs[pos:pos + n_parts] if has_bias else ()
    pos += n_parts if has_bias else 0
    job_src = refs[pos:pos + n_jobs]
    pos += n_jobs
    op_refs = refs[pos:pos + n_out]
    os_refs = refs[pos + n_out:pos + 2 * n_out]
    pos += 2 * n_out
    job_dst = refs[pos:pos + n_jobs]
    xnp_ref, xns_ref = refs[pos + n_jobs:]
    i, j = pl.program_id(0), pl.program_id(1)
    last_i = pl.num_programs(0) - 1

    @pl.when(j == 0)
    def _():
        _rms_to_bf16(xp_ref, g_ref, xnp_ref)

    @pl.when((j == 0) & (i == last_i))
    def _():
        _rms_to_bf16(xs_ref, g_ref, xns_ref)

    def rows(xn_ref, o_refs):
        parts = []
        for p in range(n_parts):
            a = jnp.dot(xn_ref[...], w_refs[p][...], preferred_element_type=F32)
            if has_bias:
                a = a + b_refs[p][...]
            parts.append(a)
        for o_ref, o in zip(o_refs, epilogue(*parts)):
            o_ref[...] = o.astype(o_ref.dtype)

    rows(xnp_ref, op_refs)
    _run_cast_jobs(job_src, job_dst)

    @pl.when(i == last_i)
    def _():
        rows(xns_ref, os_refs)


def _mm_parts(name, xp, xs, g, w, layer, bias, n_parts, out_dtypes, epilogue, tn, cast_jobs=()):
    mp, ms = xp.shape[0], xs.shape[0]
    nj = D_MODEL // tn
    ni = mp // TM
    in_specs = [pl.BlockSpec((TM, D_MODEL), lambda i, j: (i, 0)),
                pl.BlockSpec((ms, D_MODEL), lambda i, j: (0, 0)),
                pl.BlockSpec((1, D_MODEL), lambda i, j: (0, 0))]
    args = [xp, xs, _vec(g)]
    for p in range(n_parts):
        in_specs.append(_wspec((D_MODEL, tn), lambda i, j, p=p: (0, p * nj + j), layer))
        args.append(w)
    if bias is not None:
        for p in range(n_parts):
            in_specs.append(pl.BlockSpec((1, tn), lambda i, j, p=p: (0, p * nj + j)))
            args.append(bias.reshape(1, n_parts * D_MODEL))
    j_in, j_args, j_out, j_shapes = _cast_job_specs(cast_jobs, ni * nj, lambda i, j: i * nj + j)
    kern = functools.partial(_mm_parts_kernel, n_parts=n_parts, n_out=len(out_dtypes),
                             has_bias=bias is not None, n_jobs=len(cast_jobs), epilogue=epilogue)
    outs = pl.pallas_call(
        kern,
        grid=(ni, nj),
        in_specs=in_specs + j_in,
        out_specs=([pl.BlockSpec((TM, tn), lambda i, j: (i, j)) for _ in out_dtypes]
                   + [pl.BlockSpec((ms, tn), lambda i, j: (0, _sample_col(i, j, ni)))
                      for _ in out_dtypes] + j_out),
        out_shape=([jax.ShapeDtypeStruct((mp, D_MODEL), dt) for dt in out_dtypes]
                   + [jax.ShapeDtypeStruct((ms, D_MODEL), dt) for dt in out_dtypes] + j_shapes),
        scratch_shapes=[pltpu.VMEM((TM, D_MODEL), BF16), pltpu.VMEM((ms, D_MODEL), BF16)],
        compiler_params=_params("arbitrary", "arbitrary"),
        name=name,
    )(*args, *j_args)
    n = len(out_dtypes)
    return outs[:n], outs[n:2 * n], outs[2 * n:]


def _glu_epilogue(a, b):
    return (a * jax.nn.sigmoid(b),)


def _gelu_epilogue(a, b):
    return (jax.nn.gelu(a), jax.nn.gelu(b))


def _mm_res_kernel(ap_ref, as_ref, w_ref, xp_ref, xs_ref, op_ref, os_ref):
    op_ref[...] = xp_ref[...] + jnp.dot(ap_ref[...], w_ref[...], preferred_element_type=F32)

    @pl.when(pl.program_id(0) == pl.num_programs(0) - 1)
    def _():
        os_ref[...] = xs_ref[...] + jnp.dot(as_ref[...], w_ref[...], preferred_element_type=F32)


def _mm_res(name, ap, as_, w, xp, xs):
    mp, ms = ap.shape[0], as_.shape[0]
    rowp = pl.BlockSpec((TM_FULL, D_MODEL), lambda i: (i, 0))
    rows = pl.BlockSpec((ms, D_MODEL), lambda i: (0, 0))
    return pl.pallas_call(
        _mm_res_kernel,
        grid=(mp // TM_FULL,),
        in_specs=[rowp, rows,
                  pl.BlockSpec((D_MODEL, D_MODEL), lambda i: (0, 0), pipeline_mode=pl.Buffered(1)),
                  rowp, rows],
        out_specs=[rowp, rows],
        out_shape=[jax.ShapeDtypeStruct((mp, D_MODEL), F32),
                   jax.ShapeDtypeStruct((ms, D_MODEL), F32)],
        compiler_params=_params("arbitrary"),
        name=name,
    )(ap, as_, w, xp, xs)


def _mlp_kernel(*refs, final, n_jobs):
    n_in = 6 if final else 5
    xp_ref, xs_ref, g_ref, w1_ref, w2_ref = refs[:5]
    gf_ref = refs[5] if final else None
    job_src = refs[n_in:n_in + n_jobs]
    op_ref, os_ref = refs[n_in + n_jobs:n_in + n_jobs + 2]
    job_dst = refs[n_in + n_jobs + 2:n_in + 2 * n_jobs + 2]
    xnp_ref, xns_ref = refs[n_in + 2 * n_jobs + 2:]
    i, j = pl.program_id(0), pl.program_id(1)
    last_i = pl.num_programs(0) - 1
    last_j = pl.num_programs(1) - 1

    @pl.when(j == 0)
    def _():
        _rms_to_bf16(xp_ref, g_ref, xnp_ref, copy_ref=op_ref)

    @pl.when((j == 0) & (i == last_i))
    def _():
        _rms_to_bf16(xs_ref, g_ref, xns_ref, copy_ref=os_ref)

    def rows(xn_ref, o_ref):
        h = jnp.dot(xn_ref[...], w1_ref[...], preferred_element_type=F32)
        h = jnp.square(jnp.maximum(h, 0.0)).astype(BF16)
        o_ref[...] += jnp.dot(h, w2_ref[...], preferred_element_type=F32)

    rows(xnp_ref, op_ref)
    _run_cast_jobs(job_src, job_dst)

    @pl.when(i == last_i)
    def _():
        rows(xns_ref, os_ref)

    if final:
        def norm(o_ref):
            n = o_ref.shape[0]
            rc = _row_chunk(n)

            def body(r0):
                o_ref[pl.ds(r0, rc), :] = _rms_rows(o_ref[pl.ds(r0, rc), :], gf_ref[...])
            _for_chunks(n, rc, body, unroll=2)

        @pl.when(j == last_j)
        def _():
            norm(op_ref)

        @pl.when((j == last_j) & (i == last_i))
        def _():
            norm(os_ref)


def _mlp(name, xp, xs, g, w1, w2, layer, tf, final_g=None, cast_jobs=()):
    mp, ms = xp.shape[0], xs.shape[0]
    final = final_g is not None
    ni, nj = mp // TM, D_FF // tf
    in_specs = [pl.BlockSpec((TM, D_MODEL), lambda i, j: (i, 0)),
                pl.BlockSpec((ms, D_MODEL), lambda i, j: (0, 0)),
                pl.BlockSpec((1, D_MODEL), lambda i, j: (0, 0)),
                _wspec((D_MODEL, tf), lambda i, j: (0, j), layer),
                _wspec((tf, D_MODEL), lambda i, j: (j, 0), layer)]
    args = [xp, xs, _vec(g), w1, w2]
    if final:
        in_specs.append(pl.BlockSpec((1, D_MODEL), lambda i, j: (0, 0)))
        args.append(_vec(final_g))
    j_in, j_args, j_out, j_shapes = _cast_job_specs(cast_jobs, ni * nj, lambda i, j: i * nj + j)
    outs = pl.pallas_call(
        functools.partial(_mlp_kernel, final=final, n_jobs=len(cast_jobs)),
        grid=(ni, nj),
        in_specs=in_specs + j_in,
        out_specs=[pl.BlockSpec((TM, D_MODEL), lambda i, j: (i, 0)),
                   pl.BlockSpec((ms, D_MODEL), lambda i, j: (0, 0))] + j_out,
        out_shape=[jax.ShapeDtypeStruct((mp, D_MODEL), F32),
                   jax.ShapeDtypeStruct((ms, D_MODEL), F32)] + j_shapes,
        scratch_shapes=[pltpu.VMEM((TM, D_MODEL), BF16), pltpu.VMEM((ms, D_MODEL), BF16)],
        compiler_params=_params("arbitrary", "arbitrary"),
        name=name,
    )(*args, *j_args)
    return outs[0], outs[1], outs[2:]


CONV_ROWS = 128
CONV_LANES = 256


def _carry_halo(ext_ref, t, tt, halo):
    @pl.when(t == 0)
    def _():
        ext_ref[pl.ds(0, halo), :] = jnp.zeros((halo, D_MODEL), F32)

    @pl.when(t > 0)
    def _():
        ext_ref[pl.ds(0, halo), :] = ext_ref[pl.ds(tt, halo), :]


def _conv_a_kernel(*refs, tt, n_jobs):
    g_ref, w_ref, bdw_ref, lng_ref, lnb_ref = refs[:5]
    job_src = refs[5:5 + n_jobs]
    o_ref = refs[5 + n_jobs]
    job_dst = refs[6 + n_jobs:6 + 2 * n_jobs]
    ext_ref, c_ref = refs[6 + 2 * n_jobs:]
    _run_cast_jobs(job_src, job_dst)
    _carry_halo(ext_ref, pl.program_id(1), tt, HALO)

    def copy_in(r0):
        ext_ref[pl.ds(HALO + r0, 64), :] = g_ref[0, pl.ds(r0, 64), :]
    _for_chunks(tt, 64, copy_in)

    off = HALO - (CONV_A_WIDTH - 1)
    win = CONV_ROWS + HALO
    tiles = CONV_ROWS // SUBLANES
    for c0 in range(0, D_MODEL, CONV_LANES):
        lanes = pl.ds(c0, CONV_LANES)

        def rows(r0, lanes=lanes):
            w = ext_ref[pl.ds(r0, win), lanes]
            acc = jnp.zeros((tiles, SUBLANES, CONV_LANES), F32)
            for s in range(SUBLANES):
                ws = w if s == 0 else pltpu.roll(w, win - s, 0)
                for k in range(CONV_A_WIDTH):
                    if (off + k) % SUBLANES == s:
                        a = off + k - s
                        tap = ws[a:a + CONV_ROWS].reshape(tiles, SUBLANES, CONV_LANES)
                        acc = acc + tap * w_ref[k, :, lanes][None]
            c_ref[pl.ds(r0, CONV_ROWS), lanes] = acc.reshape(CONV_ROWS, CONV_LANES)
        _for_chunks(tt, CONV_ROWS, rows)

    rc = 32

    def norm(r0):
        c = c_ref[pl.ds(r0, rc), :] + bdw_ref[...]
        y = _ln_rows(c, lng_ref[...], lnb_ref[...])
        o_ref[0, pl.ds(r0, rc), :] = (y * jax.nn.sigmoid(y)).astype(BF16)
    _for_chunks(tt, rc, norm, unroll=2)


def _pad_rows(w, rows):
    return jnp.zeros((rows, D_MODEL), F32).at[:w.shape[0]].set(w)


def _conv_a_prompt(g3, w_dw, b_dw, ln_g, ln_b, cast_jobs=()):
    b, l, _ = g3.shape
    nt = l // TT
    vspec = pl.BlockSpec((1, D_MODEL), lambda i, t: (0, 0))
    w_rep = jnp.broadcast_to(_pad_rows(w_dw, 32)[:, None, :], (32, SUBLANES, D_MODEL))
    j_in, j_args, j_out, j_shapes = _cast_job_specs(cast_jobs, b * nt, lambda i, t: i * nt + t)
    outs = pl.pallas_call(
        functools.partial(_conv_a_kernel, tt=TT, n_jobs=len(cast_jobs)),
        grid=(b, nt),
        in_specs=[pl.BlockSpec((1, TT, D_MODEL), lambda i, t: (i, t, 0)),
                  pl.BlockSpec((32, SUBLANES, D_MODEL), lambda i, t: (0, 0, 0)),
                  vspec, vspec, vspec] + j_in,
        out_specs=[pl.BlockSpec((1, TT, D_MODEL), lambda i, t: (i, t, 0))] + j_out,
        out_shape=[jax.ShapeDtypeStruct((b, l, D_MODEL), BF16)] + j_shapes,
        scratch_shapes=[pltpu.VMEM((TT + HALO, D_MODEL), F32),
                        pltpu.VMEM((TT, D_MODEL), F32)],
        compiler_params=_params("arbitrary", "arbitrary"),
        name="conv_a_prompt",
    )(g3, w_rep, _vec(b_dw), _vec(ln_g), _vec(ln_b), *j_args)
    return outs[0], outs[1:]


def _conv_a_sample_kernel(st_ref, g_ref, w_ref, bdw_ref, lng_ref, lnb_ref, o_ref, ns_ref):
    n_st = CONV_A_WIDTH - 1
    acc = g_ref[...] * w_ref[pl.ds(n_st, 1), :]
    for k in range(n_st):
        acc = acc + st_ref[:, k, :] * w_ref[pl.ds(k, 1), :]
    y = _ln_rows(acc + bdw_ref[...], lng_ref[...], lnb_ref[...])
    o_ref[...] = (y * jax.nn.sigmoid(y)).astype(BF16)
    ns_ref[:, pl.ds(0, n_st - 1), :] = st_ref[:, pl.ds(1, n_st - 1), :]
    ns_ref[:, n_st - 1, :] = g_ref[...]


def _conv_a_sample(state, layer, g, w_dw, b_dw, ln_g, ln_b, tb):
    n = g.shape[0]
    n_st = CONV_A_WIDTH - 1
    vspec = pl.BlockSpec((1, D_MODEL), lambda i: (0, 0))
    st_spec = pl.BlockSpec((None, tb, n_st, D_MODEL), lambda i: (layer, i, 0, 0))
    return pl.pallas_call(
        _conv_a_sample_kernel,
        grid=(n // tb,),
        in_specs=[st_spec,
                  pl.BlockSpec((tb, D_MODEL), lambda i: (i, 0)),
                  pl.BlockSpec((32, D_MODEL), lambda i: (0, 0)),
                  vspec, vspec, vspec],
        out_specs=[pl.BlockSpec((tb, D_MODEL), lambda i: (i, 0)),
                   pl.BlockSpec((None, tb, n_st, D_MODEL), lambda i: (0, i, 0, 0))],
        out_shape=[jax.ShapeDtypeStruct((n, D_MODEL), BF16),
                   jax.ShapeDtypeStruct((1, n, n_st, D_MODEL), F32)],
        compiler_params=_params("arbitrary"),
        name="conv_a_sample",
    )(state, g, _pad_rows(w_dw, 32), _vec(b_dw), _vec(ln_g), _vec(ln_b))


POOL_HALO = 16


def _pool_prompt_kernel(x_ref, g_ref, w_ref, sc_ref, o_ref, hs_ref, ext_ref, diff_ref, wb_ref,
                        *, tt):
    t = pl.program_id(1)
    H = POOL_HALO
    _carry_halo(ext_ref, t, tt, H)

    rc = 32

    def norm(r0):
        ext_ref[pl.ds(H + r0, rc), :] = _rms_rows(x_ref[0, pl.ds(r0, rc), :], g_ref[...])
    _for_chunks(tt, rc, norm, unroll=2)

    hs_ref[0, 0] = ext_ref[pl.ds(tt, H), :]

    @pl.when((pl.program_id(0) == 0) & (t == 0))
    def _():
        wb_ref[...] = w_ref[...].astype(BF16)

    pr = 64
    win = pr + H
    for gi, w in enumerate(POOL_WINDOWS):
        lanes = pl.ds(gi * POOL_GROUP, POOL_GROUP)

        def rows(r0, w=w, lanes=lanes):
            s = ext_ref[pl.ds(r0, win), lanes]
            cur = s[H:]
            d = 1
            while d < w:
                s = s + pltpu.roll(s, d, 0)
                d *= 2
            pos = t * tt + r0 + lax.broadcasted_iota(jnp.int32, (pr, 1), 0)
            cnt = jnp.minimum(pos + 1, w).astype(F32)
            diff_ref[pl.ds(r0, pr), lanes] = (s[H:] / cnt - cur).astype(BF16)
        _for_chunks(tt, pr, rows)

    for gi in range(len(POOL_WINDOWS)):
        sl = slice(gi * POOL_GROUP, (gi + 1) * POOL_GROUP)
        y = jnp.dot(diff_ref[:, sl], wb_ref[gi], preferred_element_type=F32)
        o_ref[0, :, sl] = x_ref[0, :, sl] + y * sc_ref[:, sl]


def _pool_prompt(x3, g, w_grp, layer, scale):
    b, l, _ = x3.shape
    nt = l // TT
    ng = len(POOL_WINDOWS)
    vspec = pl.BlockSpec((1, D_MODEL), lambda i, t: (0, 0))
    return pl.pallas_call(
        functools.partial(_pool_prompt_kernel, tt=TT),
        grid=(b, nt),
        in_specs=[pl.BlockSpec((1, TT, D_MODEL), lambda i, t: (i, t, 0)),
                  vspec,
                  pl.BlockSpec((None, ng, POOL_GROUP, POOL_GROUP), lambda i, t: (layer, 0, 0, 0)),
                  vspec],
        out_specs=[pl.BlockSpec((1, TT, D_MODEL), lambda i, t: (i, t, 0)),
                   pl.BlockSpec((1, 1, POOL_HALO, D_MODEL), lambda i, t: (i, t, 0, 0))],
        out_shape=[jax.ShapeDtypeStruct((b, l, D_MODEL), F32),
                   jax.ShapeDtypeStruct((b, nt, POOL_HALO, D_MODEL), F32)],
        scratch_shapes=[pltpu.VMEM((TT + POOL_HALO, D_MODEL), F32),
                        pltpu.VMEM((TT, D_MODEL), BF16),
                        pltpu.VMEM((ng, POOL_GROUP, POOL_GROUP), BF16)],
        compiler_params=_params("arbitrary", "arbitrary"),
        name="pool_prompt",
    )(x3, _vec(g), w_grp, _vec(scale))


def _pool_sample_kernel(x_ref, st_ref, g_ref, w_ref, sc_ref, o_ref, ns_ref):
    x = x_ref[...]
    h = _rms_rows(x, g_ref[...])
    for gi, w in enumerate(POOL_WINDOWS):
        sl = slice(gi * POOL_GROUP, (gi + 1) * POOL_GROUP)
        cur = h[:, sl]
        s = cur
        for d in range(1, w):
            s = s + st_ref[:, POOL_STATE - d, sl]
        diff = s / float(min(PAST_LEN + 1, w)) - cur
        y = jnp.dot(diff.astype(BF16), w_ref[gi].astype(BF16), preferred_element_type=F32)
        o_ref[:, sl] = x[:, sl] + y * sc_ref[:, sl]
    ns_ref[:, pl.ds(0, POOL_STATE - 1), :] = st_ref[:, pl.ds(1, POOL_STATE - 1), :]
    ns_ref[:, POOL_STATE - 1, :] = h


def _pool_sample(x, state, g, w_grp, layer, scale, tb):
    n = x.shape[0]
    ng = len(POOL_WINDOWS)
    vspec = pl.BlockSpec((1, D_MODEL), lambda i: (0, 0))
    row = pl.BlockSpec((tb, D_MODEL), lambda i: (i, 0))
    return pl.pallas_call(
        _pool_sample_kernel,
        grid=(n // tb,),
        in_specs=[row,
                  pl.BlockSpec((None, tb, POOL_STATE, D_MODEL), lambda i: (layer, i, 0, 0)),
                  vspec,
                  pl.BlockSpec((None, ng, POOL_GROUP, POOL_GROUP), lambda i: (layer, 0, 0, 0)),
                  vspec],
        out_specs=[row, pl.BlockSpec((None, tb, POOL_STATE, D_MODEL), lambda i: (0, i, 0, 0))],
        out_shape=[jax.ShapeDtypeStruct((n, D_MODEL), F32),
                   jax.ShapeDtypeStruct((1, n, POOL_STATE, D_MODEL), F32)],
        compiler_params=_params("arbitrary"),
        name="pool_sample",
    )(x, state, _vec(g), w_grp, _vec(scale))


C_ROWS = 128


def _c_in_kernel(xp_ref, xs_ref, g_ref, wb_ref, wc_ref, wx_ref, wconv_ref,
                 z_ref, tail_ref, bgs_ref, cxs_ref, xnp_ref, xns_ref, ext_ref, carry_ref,
                 *, blocks_per_seq):
    i, j = pl.program_id(0), pl.program_id(1)
    last_i = pl.num_programs(0) - 1
    H = SUBLANES
    tm = xp_ref.shape[0]

    @pl.when(j == 0)
    def _():
        _rms_to_bf16(xp_ref, g_ref, xnp_ref)

    @pl.when((j == 0) & (i == last_i))
    def _():
        _rms_to_bf16(xs_ref, g_ref, xns_ref)

    def parts(xn_ref):
        xn = xn_ref[...]
        return tuple(jnp.dot(xn, w_ref[...], preferred_element_type=F32)
                     for w_ref in (wb_ref, wc_ref, wx_ref))

    bg, c, xv = parts(xnp_ref)
    cx = c * xv
    prev = carry_ref[j]
    ext_ref[pl.ds(0, H), :] = jnp.where(i % blocks_per_seq == 0, jnp.zeros_like(prev), prev)
    ext_ref[pl.ds(H, tm), :] = cx
    carry_ref[j] = cx[tm - H:]
    tail_ref[0] = cx[tm - H:]
    win = C_ROWS + H
    for r0 in range(0, tm, C_ROWS):
        w = ext_ref[pl.ds(r0, win), :]
        y = (pltpu.roll(w, 2, 0)[H:] * wconv_ref[pl.ds(0, 1), :]
             + pltpu.roll(w, 1, 0)[H:] * wconv_ref[pl.ds(1, 1), :]
             + w[H:] * wconv_ref[pl.ds(2, 1), :])
        z_ref[pl.ds(r0, C_ROWS), :] = (bg[r0:r0 + C_ROWS] * y).astype(BF16)

    @pl.when(i == last_i)
    def _():
        bs, cs, xvs = parts(xns_ref)
        bgs_ref[...] = bs
        cxs_ref[...] = cs * xvs


def _c_in_conv(xp, xs, g, w, w_conv, seq_len, tn):
    mp, ms = xp.shape[0], xs.shape[0]
    nj = D_MODEL // tn
    ni = mp // TM
    assert seq_len % TM == 0
    wspecs = [pl.BlockSpec((D_MODEL, tn), lambda i, j, p=p: (0, p * nj + j)) for p in range(3)]
    tile_s = pl.BlockSpec((ms, tn), lambda i, j: (0, _sample_col(i, j, ni)))
    return pl.pallas_call(
        functools.partial(_c_in_kernel, blocks_per_seq=seq_len // TM),
        grid=(ni, nj),
        in_specs=[pl.BlockSpec((TM, D_MODEL), lambda i, j: (i, 0)),
                  pl.BlockSpec((ms, D_MODEL), lambda i, j: (0, 0)),
                  pl.BlockSpec((1, D_MODEL), lambda i, j: (0, 0))] + wspecs
                 + [pl.BlockSpec((SUBLANES, tn), lambda i, j: (0, j))],
        out_specs=[pl.BlockSpec((TM, tn), lambda i, j: (i, j)),
                   pl.BlockSpec((1, SUBLANES, tn), lambda i, j: (i, 0, j)),
                   tile_s, tile_s],
        out_shape=[jax.ShapeDtypeStruct((mp, D_MODEL), BF16),
                   jax.ShapeDtypeStruct((ni, SUBLANES, D_MODEL), F32),
                   jax.ShapeDtypeStruct((ms, D_MODEL), F32),
                   jax.ShapeDtypeStruct((ms, D_MODEL), F32)],
        scratch_shapes=[pltpu.VMEM((TM, D_MODEL), BF16), pltpu.VMEM((ms, D_MODEL), BF16),
                        pltpu.VMEM((TM + SUBLANES, tn), F32),
                        pltpu.VMEM((nj, SUBLANES, tn), F32)],
        compiler_params=_params("arbitrary", "arbitrary"),
        name="c_in_conv",
    )(xp, xs, _vec(g), w, w, w, _pad_rows(w_conv, SUBLANES))


def _conv_c_sample_kernel(b_ref, cx_ref, st_ref, w_ref, o_ref, ns_ref):
    y = (st_ref[:, 0, :] * w_ref[pl.ds(0, 1), :] + st_ref[:, 1, :] * w_ref[pl.ds(1, 1), :]
         + cx_ref[...] * w_ref[pl.ds(2, 1), :])
    o_ref[...] = (b_ref[...] * y).astype(BF16)
    ns_ref[:, 0, :] = st_ref[:, 1, :]
    ns_ref[:, 1, :] = cx_ref[...]


def _conv_c_sample(bg, cx, state, layer, w_conv):
    n = bg.shape[0]
    n_st = CONV_C_WIDTH - 1
    row = pl.BlockSpec((n, D_MODEL), lambda i: (0, 0))
    return pl.pallas_call(
        _conv_c_sample_kernel,
        grid=(1,),
        in_specs=[row, row,
                  pl.BlockSpec((None, n, n_st, D_MODEL), lambda i: (layer, 0, 0, 0)),
                  pl.BlockSpec((SUBLANES, D_MODEL), lambda i: (0, 0))],
        out_specs=[row, pl.BlockSpec((None, n, n_st, D_MODEL), lambda i: (0, 0, 0, 0))],
        out_shape=[jax.ShapeDtypeStruct((n, D_MODEL), BF16),
                   jax.ShapeDtypeStruct((1, n, n_st, D_MODEL), F32)],
        compiler_params=_params("arbitrary"),
        name="conv_c_sample",
    )(bg, cx, state, _pad_rows(w_conv, SUBLANES))


def _sg_kernel(up_ref, vp_ref, us_ref, vs_ref, lng_ref, lnb_ref, ws_ref, bs_ref, ws0_ref, bs0_ref,
               wo_ref, xp_ref, xs_ref, op_ref, os_ref, vns_ref, vn_ref, gt_ref, wsm_ref):
    i = pl.program_id(0)
    tm = up_ref.shape[0]

    @pl.when(i == 0)
    def _():
        row = lax.broadcasted_iota(jnp.int32, (CHUNK, CHUNK), 0)
        col = lax.broadcasted_iota(jnp.int32, (CHUNK, CHUNK), 1)
        for hd in range(N_SG_HEADS):
            wsm_ref[hd] = jnp.where(col <= row, ws_ref[hd], 0.0).astype(BF16)

    rc = 32

    def norm(r0):
        vn_ref[pl.ds(r0, rc), :] = _ln_rows(vp_ref[pl.ds(r0, rc), :], lng_ref[...],
                                            lnb_ref[...]).astype(BF16)
    _for_chunks(tm, rc, norm, unroll=2)

    for r0 in range(0, tm, CHUNK):
        for hd in range(N_SG_HEADS):
            lanes = pl.ds(hd * SG_HEAD_DIM, SG_HEAD_DIM)
            mixed = jnp.dot(wsm_ref[hd], vn_ref[pl.ds(r0, CHUNK), lanes],
                            preferred_element_type=F32) + bs_ref[:, lanes]
            gt_ref[pl.ds(r0, CHUNK), lanes] = (up_ref[pl.ds(r0, CHUNK), lanes].astype(F32)
                                               * mixed).astype(BF16)

    op_ref[...] = xp_ref[...] + jnp.dot(gt_ref[...], wo_ref[...], preferred_element_type=F32)

    @pl.when(i == pl.num_programs(0) - 1)
    def _():
        vn = _ln_rows(vs_ref[...], lng_ref[...], lnb_ref[...])
        vns_ref[...] = vn
        mixed = ws0_ref[...] * vn + bs0_ref[...]
        gated = (us_ref[...].astype(F32) * mixed).astype(BF16)
        os_ref[...] = xs_ref[...] + jnp.dot(gated, wo_ref[...], preferred_element_type=F32)


def _sg(up, vp, us, vs, ln_g, ln_b, w_s, layer, bs_map, ws0_map, bs0_map, w_o, xp, xs):
    mp, ms = up.shape[0], us.shape[0]
    vspec = pl.BlockSpec((1, D_MODEL), lambda i: (0, 0))
    rowp = pl.BlockSpec((TM_FULL, D_MODEL), lambda i: (i, 0))
    rows = pl.BlockSpec((ms, D_MODEL), lambda i: (0, 0))
    return pl.pallas_call(
        _sg_kernel,
        grid=(mp // TM_FULL,),
        in_specs=[rowp, rowp, rows, rows, vspec, vspec,
                  pl.BlockSpec((None, N_SG_HEADS, CHUNK, CHUNK), lambda i: (layer, 0, 0, 0)),
                  pl.BlockSpec((CHUNK, D_MODEL), lambda i: (0, 0)),
                  vspec, vspec,
                  pl.BlockSpec((D_MODEL, D_MODEL), lambda i: (0, 0), pipeline_mode=pl.Buffered(1)),
                  rowp, rows],
        out_specs=[rowp, rows, rows],
        out_shape=[jax.ShapeDtypeStruct((mp, D_MODEL), F32),
                   jax.ShapeDtypeStruct((ms, D_MODEL), F32),
                   jax.ShapeDtypeStruct((ms, D_MODEL), F32)],
        scratch_shapes=[pltpu.VMEM((TM_FULL, D_MODEL), BF16), pltpu.VMEM((TM_FULL, D_MODEL), BF16),
                        pltpu.VMEM((N_SG_HEADS, CHUNK, CHUNK), BF16)],
        compiler_params=_params("arbitrary"),
        name="sg_mix",
    )(up, vp, us, vs, _vec(ln_g), _vec(ln_b), w_s, bs_map, ws0_map, bs0_map, w_o, xp, xs)


def kernel(x_prompt, x_sample, state_conv_a, state_pool, state_short_conv, norm_mix, norm_mlp, norm_final, a_w_pw1, a_b_pw1, a_w_dw, a_b_dw, a_ln_g, a_ln_b, a_w_pw2, b_w_grp, b_scale, c_w_in, c_w_conv, c_w_out, d_w_uv, d_ln_g, d_ln_b, d_w_s, d_b_s, d_w_o, mlp_w1, mlp_w2):
    b, l, _ = x_prompt.shape
    mp = b * l
    ms = x_sample.shape[0]
    assert x_sample.shape[1] == 1 and mp % TM == 0 and l % TT == 0
    flat3 = lambda a: a.reshape(b, l, D_MODEL)
    xp = x_prompt.reshape(mp, D_MODEL)
    xs = x_sample.reshape(ms, D_MODEL)
    tf = 512

    (gp,), (gs,), (w_pw2,) = _mm_parts(
        "a_pw1_glu", xp, xs, norm_mix[0], a_w_pw1.astype(BF16), 0, a_b_pw1[0], 2, (F32,),
        _glu_epilogue, 512, cast_jobs=[(a_w_pw2, 0)])
    cp3, (w1_0, w2_0) = _conv_a_prompt(flat3(gp), a_w_dw[0], a_b_dw[0], a_ln_g[0], a_ln_b[0],
                                       cast_jobs=[(mlp_w1, 0), (mlp_w2, 0)])
    cp = cp3.reshape(mp, D_MODEL)
    cs, sa = _conv_a_sample(state_conv_a, 0, gs, a_w_dw[0], a_b_dw[0], a_ln_g[0], a_ln_b[0], 32)
    pa = flat3(gp)[:, l - (CONV_A_WIDTH - 1):]
    xp, xs = _mm_res("a_pw2", cp, cs, w_pw2, xp, xs)
    xp, xs, (w1_1, w2_1, w_in, w_out) = _mlp(
        "mlp0", xp, xs, norm_mlp[0], w1_0, w2_0, None, tf,
        cast_jobs=[(mlp_w1, 1), (mlp_w2, 1), (c_w_in, 0), (c_w_out, 0)])

    xp3, hs = _pool_prompt(flat3(xp), norm_mix[1], b_w_grp, 0, b_scale[0])
    xp = xp3.reshape(mp, D_MODEL)
    pb = hs[:, -1, POOL_HALO - POOL_STATE:]
    xs, sb = _pool_sample(xs, state_pool, norm_mix[1], b_w_grp, 0, b_scale[0], 32)
    xp, xs, (w1_2, w2_2, w_uv, w_o) = _mlp(
        "mlp1", xp, xs, norm_mlp[1], w1_1, w2_1, None, tf,
        cast_jobs=[(mlp_w1, 2), (mlp_w2, 2), (d_w_uv, 0), (d_w_o, 0)])

    zp, cx_tail, bgs, cxs = _c_in_conv(xp, xs, norm_mix[2], w_in, c_w_conv[0], l, 256)
    zs, sc = _conv_c_sample(bgs, cxs, state_short_conv, 0, c_w_conv[0])
    pc = cx_tail.reshape(b, l // TM, SUBLANES, D_MODEL)[:, -1, SUBLANES - (CONV_C_WIDTH - 1):]
    xp, xs = _mm_res("c_out", zp, zs, w_out, xp, xs)
    xp, xs, (w1_3, w2_3) = _mlp("mlp2", xp, xs, norm_mlp[2], w1_2, w2_2, None, tf,
                                cast_jobs=[(mlp_w1, 3), (mlp_w2, 3)])

    (up, vp), (us, vs), _ = _mm_parts("d_uv_gelu", xp, xs, norm_mix[3], w_uv, None, None, 2,
                                      (BF16, F32), _gelu_epilogue, 1024)
    bs_map = jnp.repeat(d_b_s[0].T, SG_HEAD_DIM, axis=1)
    ws0_map = _vec(jnp.repeat(d_w_s[0, :, 0, 0], SG_HEAD_DIM))
    bs0_map = _vec(jnp.repeat(d_b_s[0, :, 0], SG_HEAD_DIM))
    xp, xs, sd = _sg(up, vp, us, vs, d_ln_g[0], d_ln_b[0], d_w_s, 0, bs_map, ws0_map, bs0_map,
                     w_o, xp, xs)
    yp, ys, _ = _mlp("mlp3", xp, xs, norm_mlp[3], w1_3, w2_3, None, tf, final_g=norm_final)

    return (yp.reshape(b, l, D_MODEL), ys.reshape(ms, 1, D_MODEL),
            pa[None], sa, pb[None], sb, pc[None], sc,
            sd.reshape(1, ms, 1, D_MODEL))
```

```python
import functools

import jax
import jax.numpy as jnp
from jax import lax
from jax.experimental import pallas as pl
from jax.experimental.pallas import tpu as pltpu

D_MODEL = 2048
D_FF = 4 * D_MODEL
CONV_A_WIDTH = 31
POOL_WINDOWS = (2, 4, 8, 16)
POOL_GROUP = D_MODEL // len(POOL_WINDOWS)
POOL_STATE = max(POOL_WINDOWS) - 1
CONV_C_WIDTH = 3
CHUNK = 128
N_SG_HEADS = 8
SG_HEAD_DIM = D_MODEL // N_SG_HEADS
PAST_LEN = 16384
RMS_EPS = 1e-6
LN_EPS = 1e-5

SUBLANES = 8
BF16_ROWS = 16
VMEM_LIMIT_BYTES = 56 * 1024 * 1024
HALO = 32

TM = 1024
TM_FULL = 512
TT = 256

F32 = jnp.float32
BF16 = jnp.bfloat16


def _params(*sem):
    return pltpu.CompilerParams(dimension_semantics=sem, vmem_limit_bytes=VMEM_LIMIT_BYTES)


def _for_chunks(n, chunk, body, unroll=1):
    assert n % chunk == 0
    if n == chunk:
        body(0)
    else:
        def step(i, c):
            body(pl.multiple_of(i * chunk, chunk))
            return c
        lax.fori_loop(0, n // chunk, step, 0, unroll=unroll)


def _rms_rows(x, g):
    r = lax.rsqrt(jnp.mean(x * x, axis=-1, keepdims=True) + RMS_EPS)
    return x * r * g


def _ln_rows(x, g, b):
    mu = jnp.mean(x, axis=-1, keepdims=True)
    xc = x - mu
    var = jnp.mean(xc * xc, axis=-1, keepdims=True)
    return xc * lax.rsqrt(var + LN_EPS) * g + b


def _row_chunk(tm):
    for c in (64, 32, 16, 8):
        if tm % c == 0:
            return c
    raise ValueError(tm)


def _rms_to_bf16(x_ref, g_ref, xn_ref, copy_ref=None):
    n = x_ref.shape[0]
    rc = _row_chunk(n)

    def body(r0):
        x = x_ref[pl.ds(r0, rc), :]
        xn_ref[pl.ds(r0, rc), :] = _rms_rows(x, g_ref[...]).astype(BF16)
        if copy_ref is not None:
            copy_ref[pl.ds(r0, rc), :] = x
    _for_chunks(n, rc, body, unroll=2)


def _vec(v):
    return v.reshape(1, D_MODEL)


def _sample_col(i, j, ni):
    return jnp.where(i == ni - 1, j, 0)


def _cast_job_specs(jobs, n_steps, step_of):
    in_specs, args, out_specs, out_shapes = [], [], [], []
    for src, layer, tile in jobs:
        _, r, c = src.shape
        assert r % (n_steps * BF16_ROWS) == 0, (src.shape, n_steps)
        pr = r // n_steps
        in_specs.append(pl.BlockSpec((None, pr, c),
                                     lambda *g, layer=layer: (layer, step_of(*g), 0)))
        args.append(src)
        if tile is None:
            out_specs.append(pl.BlockSpec((pr, c), lambda *g: (step_of(*g), 0)))
            out_shapes.append(jax.ShapeDtypeStruct((r, c), BF16))
        else:
            out_specs.append(pl.BlockSpec((c // tile, pr, tile), lambda *g: (0, step_of(*g), 0)))
            out_shapes.append(jax.ShapeDtypeStruct((c // tile, r, tile), BF16))
    return in_specs, args, out_specs, out_shapes


def _run_cast_jobs(src_refs, dst_refs):
    for s_ref, d_ref in zip(src_refs, dst_refs):
        if len(d_ref.shape) == 2:
            d_ref[...] = s_ref[...].astype(BF16)
        else:
            n_tiles, _, tile = d_ref.shape
            for t in range(n_tiles):
                d_ref[t] = s_ref[:, pl.ds(t * tile, tile)].astype(BF16)


def _tile_major(w, tile):
    r, c = w.shape
    return w.reshape(r, c // tile, tile).transpose(1, 0, 2)


def _mm_parts_kernel(*refs, n_parts, n_out, has_bias, n_jobs, epilogue):
    xp_ref, xs_ref, g_ref = refs[:3]
    w_refs = refs[3:3 + n_parts]
    pos = 3 + n_parts
    b_refs = refs[pos:pos + n_parts] if has_bias else ()
    pos += n_parts if has_bias else 0
    job_src = refs[pos:pos + n_jobs]
    pos += n_jobs
    op_refs = refs[pos:pos + n_out]
    os_refs = refs[pos + n_out:pos + 2 * n_out]
    pos += 2 * n_out
    job_dst = refs[pos:pos + n_jobs]
    xnp_ref, xns_ref = refs[pos + n_jobs:]
    i, j = pl.program_id(0), pl.program_id(1)
    last_i = pl.num_programs(0) - 1

    @pl.when(j == 0)
    def _():
        _rms_to_bf16(xp_ref, g_ref, xnp_ref)

    @pl.when((j == 0) & (i == last_i))
    def _():
        _rms_to_bf16(xs_ref, g_ref, xns_ref)

    def rows(xn_ref, o_refs):
        parts = []
        for p in range(n_parts):
            a = jnp.dot(xn_ref[...], w_refs[p][...], preferred_element_type=F32)
            if has_bias:
                a = a + b_refs[p][...]
            parts.append(a)
        for o_ref, o in zip(o_refs, epilogue(*parts)):
            o_ref[...] = o.astype(o_ref.dtype)

    rows(xnp_ref, op_refs)
    _run_cast_jobs(job_src, job_dst)

    @pl.when(i == last_i)
    def _():
        rows(xns_ref, os_refs)


def _mm_parts(name, xp, xs, g, w, bias, n_parts, out_dtypes, epilogue, cast_jobs=()):
    mp, ms = xp.shape[0], xs.shape[0]
    tn = w.shape[2]
    nj = D_MODEL // tn
    assert w.shape == (n_parts * nj, D_MODEL, tn)
    ni = mp // TM
    in_specs = [pl.BlockSpec((TM, D_MODEL), lambda i, j: (i, 0)),
                pl.BlockSpec((ms, D_MODEL), lambda i, j: (0, 0)),
                pl.BlockSpec((1, D_MODEL), lambda i, j: (0, 0))]
    args = [xp, xs, _vec(g)]
    for p in range(n_parts):
        in_specs.append(pl.BlockSpec((None, D_MODEL, tn), lambda i, j, p=p: (p * nj + j, 0, 0)))
        args.append(w)
    if bias is not None:
        for p in range(n_parts):
            in_specs.append(pl.BlockSpec((1, tn), lambda i, j, p=p: (0, p * nj + j)))
            args.append(bias.reshape(1, n_parts * D_MODEL))
    j_in, j_args, j_out, j_shapes = _cast_job_specs(cast_jobs, ni * nj, lambda i, j: i * nj + j)
    kern = functools.partial(_mm_parts_kernel, n_parts=n_parts, n_out=len(out_dtypes),
                             has_bias=bias is not None, n_jobs=len(cast_jobs), epilogue=epilogue)
    outs = pl.pallas_call(
        kern,
        grid=(ni, nj),
        in_specs=in_specs + j_in,
        out_specs=([pl.BlockSpec((TM, tn), lambda i, j: (i, j)) for _ in out_dtypes]
                   + [pl.BlockSpec((ms, tn), lambda i, j: (0, _sample_col(i, j, ni)))
                      for _ in out_dtypes] + j_out),
        out_shape=([jax.ShapeDtypeStruct((mp, D_MODEL), dt) for dt in out_dtypes]
                   + [jax.ShapeDtypeStruct((ms, D_MODEL), dt) for dt in out_dtypes] + j_shapes),
        scratch_shapes=[pltpu.VMEM((TM, D_MODEL), BF16), pltpu.VMEM((ms, D_MODEL), BF16)],
        compiler_params=_params("arbitrary", "arbitrary"),
        name=name,
    )(*args, *j_args)
    n = len(out_dtypes)
    return outs[:n], outs[n:2 * n], outs[2 * n:]


def _glu_epilogue(a, b):
    return (a * jax.nn.sigmoid(b),)


def _gelu_epilogue(a, b):
    return (jax.nn.gelu(a), jax.nn.gelu(b))


def _mm_res_kernel(ap_ref, as_ref, w_ref, xp_ref, xs_ref, op_ref, os_ref):
    op_ref[...] = xp_ref[...] + jnp.dot(ap_ref[...], w_ref[...], preferred_element_type=F32)

    @pl.when(pl.program_id(0) == pl.num_programs(0) - 1)
    def _():
        os_ref[...] = xs_ref[...] + jnp.dot(as_ref[...], w_ref[...], preferred_element_type=F32)


def _mm_res(name, ap, as_, w, xp, xs):
    mp, ms = ap.shape[0], as_.shape[0]
    rowp = pl.BlockSpec((TM_FULL, D_MODEL), lambda i: (i, 0))
    rows = pl.BlockSpec((ms, D_MODEL), lambda i: (0, 0))
    return pl.pallas_call(
        _mm_res_kernel,
        grid=(mp // TM_FULL,),
        in_specs=[rowp, rows,
                  pl.BlockSpec((D_MODEL, D_MODEL), lambda i: (0, 0), pipeline_mode=pl.Buffered(1)),
                  rowp, rows],
        out_specs=[rowp, rows],
        out_shape=[jax.ShapeDtypeStruct((mp, D_MODEL), F32),
                   jax.ShapeDtypeStruct((ms, D_MODEL), F32)],
        compiler_params=_params("arbitrary"),
        name=name,
    )(ap, as_, w, xp, xs)


def _mlp_kernel(*refs, final, n_jobs):
    n_in = 6 if final else 5
    xp_ref, xs_ref, g_ref, w1_ref, w2_ref = refs[:5]
    gf_ref = refs[5] if final else None
    job_src = refs[n_in:n_in + n_jobs]
    op_ref, os_ref = refs[n_in + n_jobs:n_in + n_jobs + 2]
    job_dst = refs[n_in + n_jobs + 2:n_in + 2 * n_jobs + 2]
    xnp_ref, xns_ref = refs[n_in + 2 * n_jobs + 2:]
    i, j = pl.program_id(0), pl.program_id(1)
    last_i = pl.num_programs(0) - 1
    last_j = pl.num_programs(1) - 1

    @pl.when(j == 0)
    def _():
        _rms_to_bf16(xp_ref, g_ref, xnp_ref, copy_ref=op_ref)

    @pl.when((j == 0) & (i == last_i))
    def _():
        _rms_to_bf16(xs_ref, g_ref, xns_ref, copy_ref=os_ref)

    def rows(xn_ref, o_ref):
        h = jnp.dot(xn_ref[...], w1_ref[...], preferred_element_type=F32)
        h = jnp.square(jnp.maximum(h, 0.0)).astype(BF16)
        o_ref[...] += jnp.dot(h, w2_ref[...], preferred_element_type=F32)

    rows(xnp_ref, op_ref)
    _run_cast_jobs(job_src, job_dst)

    @pl.when(i == last_i)
    def _():
        rows(xns_ref, os_ref)

    if final:
        def norm(o_ref):
            n = o_ref.shape[0]
            rc = _row_chunk(n)

            def body(r0):
                o_ref[pl.ds(r0, rc), :] = _rms_rows(o_ref[pl.ds(r0, rc), :], gf_ref[...])
            _for_chunks(n, rc, body, unroll=2)

        @pl.when(j == last_j)
        def _():
            norm(op_ref)

        @pl.when((j == last_j) & (i == last_i))
        def _():
            norm(os_ref)


def _mlp(name, xp, xs, g, w1, w2, final_g=None, cast_jobs=()):
    mp, ms = xp.shape[0], xs.shape[0]
    tf = w1.shape[2]
    assert w1.shape == (D_FF // tf, D_MODEL, tf) and w2.shape == (D_FF, D_MODEL)
    final = final_g is not None
    ni, nj = mp // TM, D_FF // tf
    in_specs = [pl.BlockSpec((TM, D_MODEL), lambda i, j: (i, 0)),
                pl.BlockSpec((ms, D_MODEL), lambda i, j: (0, 0)),
                pl.BlockSpec((1, D_MODEL), lambda i, j: (0, 0)),
                pl.BlockSpec((None, D_MODEL, tf), lambda i, j: (j, 0, 0)),
                pl.BlockSpec((tf, D_MODEL), lambda i, j: (j, 0))]
    args = [xp, xs, _vec(g), w1, w2]
    if final:
        in_specs.append(pl.BlockSpec((1, D_MODEL), lambda i, j: (0, 0)))
        args.append(_vec(final_g))
    j_in, j_args, j_out, j_shapes = _cast_job_specs(cast_jobs, ni * nj, lambda i, j: i * nj + j)
    outs = pl.pallas_call(
        functools.partial(_mlp_kernel, final=final, n_jobs=len(cast_jobs)),
        grid=(ni, nj),
        in_specs=in_specs + j_in,
        out_specs=[pl.BlockSpec((TM, D_MODEL), lambda i, j: (i, 0)),
                   pl.BlockSpec((ms, D_MODEL), lambda i, j: (0, 0))] + j_out,
        out_shape=[jax.ShapeDtypeStruct((mp, D_MODEL), F32),
                   jax.ShapeDtypeStruct((ms, D_MODEL), F32)] + j_shapes,
        scratch_shapes=[pltpu.VMEM((TM, D_MODEL), BF16), pltpu.VMEM((ms, D_MODEL), BF16)],
        compiler_params=_params("arbitrary", "arbitrary"),
        name=name,
    )(*args, *j_args)
    return outs[0], outs[1], outs[2:]


CONV_ROWS = 128
CONV_LANES = 256


def _carry_halo(ext_ref, t, tt, halo):
    @pl.when(t == 0)
    def _():
        ext_ref[pl.ds(0, halo), :] = jnp.zeros((halo, D_MODEL), F32)

    @pl.when(t > 0)
    def _():
        ext_ref[pl.ds(0, halo), :] = ext_ref[pl.ds(tt, halo), :]


def _conv_a_kernel(*refs, tt, n_jobs):
    g_ref, w_ref, bdw_ref, lng_ref, lnb_ref = refs[:5]
    job_src = refs[5:5 + n_jobs]
    o_ref = refs[5 + n_jobs]
    job_dst = refs[6 + n_jobs:6 + 2 * n_jobs]
    ext_ref, c_ref = refs[6 + 2 * n_jobs:]
    _run_cast_jobs(job_src, job_dst)
    _carry_halo(ext_ref, pl.program_id(1), tt, HALO)

    def copy_in(r0):
        ext_ref[pl.ds(HALO + r0, 64), :] = g_ref[0, pl.ds(r0, 64), :]
    _for_chunks(tt, 64, copy_in)

    off = HALO - (CONV_A_WIDTH - 1)
    win = CONV_ROWS + HALO
    tiles = CONV_ROWS // SUBLANES
    for c0 in range(0, D_MODEL, CONV_LANES):
        lanes = pl.ds(c0, CONV_LANES)

        def rows(r0, lanes=lanes):
            w = ext_ref[pl.ds(r0, win), lanes]
            acc = jnp.zeros((tiles, SUBLANES, CONV_LANES), F32)
            for s in range(SUBLANES):
                ws = w if s == 0 else pltpu.roll(w, win - s, 0)
                for k in range(CONV_A_WIDTH):
                    if (off + k) % SUBLANES == s:
                        a = off + k - s
                        tap = ws[a:a + CONV_ROWS].reshape(tiles, SUBLANES, CONV_LANES)
                        acc = acc + tap * w_ref[k, :, lanes][None]
            c_ref[pl.ds(r0, CONV_ROWS), lanes] = acc.reshape(CONV_ROWS, CONV_LANES)
        _for_chunks(tt, CONV_ROWS, rows)

    rc = 32

    def norm(r0):
        c = c_ref[pl.ds(r0, rc), :] + bdw_ref[...]
        y = _ln_rows(c, lng_ref[...], lnb_ref[...])
        o_ref[0, pl.ds(r0, rc), :] = (y * jax.nn.sigmoid(y)).astype(BF16)
    _for_chunks(tt, rc, norm, unroll=2)


def _pad_rows(w, rows):
    return jnp.zeros((rows, D_MODEL), F32).at[:w.shape[0]].set(w)


def _conv_a_prompt(g3, w_dw, b_dw, ln_g, ln_b, cast_jobs=()):
    b, l, _ = g3.shape
    nt = l // TT
    vspec = pl.BlockSpec((1, D_MODEL), lambda i, t: (0, 0))
    w_rep = jnp.broadcast_to(_pad_rows(w_dw, 32)[:, None, :], (32, SUBLANES, D_MODEL))
    j_in, j_args, j_out, j_shapes = _cast_job_specs(cast_jobs, b * nt, lambda i, t: i * nt + t)
    outs = pl.pallas_call(
        functools.partial(_conv_a_kernel, tt=TT, n_jobs=len(cast_jobs)),
        grid=(b, nt),
        in_specs=[pl.BlockSpec((1, TT, D_MODEL), lambda i, t: (i, t, 0)),
                  pl.BlockSpec((32, SUBLANES, D_MODEL), lambda i, t: (0, 0, 0)),
                  vspec, vspec, vspec] + j_in,
        out_specs=[pl.BlockSpec((1, TT, D_MODEL), lambda i, t: (i, t, 0))] + j_out,
        out_shape=[jax.ShapeDtypeStruct((b, l, D_MODEL), BF16)] + j_shapes,
        scratch_shapes=[pltpu.VMEM((TT + HALO, D_MODEL), F32),
                        pltpu.VMEM((TT, D_MODEL), F32)],
        compiler_params=_params("arbitrary", "arbitrary"),
        name="conv_a_prompt",
    )(g3, w_rep, _vec(b_dw), _vec(ln_g), _vec(ln_b), *j_args)
    return outs[0], outs[1:]


def _state_row(st_ref, k):
    return st_ref[:, pl.ds(k * D_MODEL, D_MODEL)]


def _push_state(st_ref, new_row, ns_ref, n_st):
    if n_st > 1:
        ns_ref[:, pl.ds(0, (n_st - 1) * D_MODEL)] = st_ref[:, pl.ds(D_MODEL, (n_st - 1) * D_MODEL)]
    ns_ref[:, pl.ds((n_st - 1) * D_MODEL, D_MODEL)] = new_row


def _flat_state(state):
    l, n, r, d = state.shape
    return state.reshape(l, n, r * d)


def _conv_a_sample_kernel(st_ref, g_ref, w_ref, bdw_ref, lng_ref, lnb_ref, o_ref, ns_ref):
    n_st = CONV_A_WIDTH - 1
    acc = g_ref[...] * w_ref[pl.ds(n_st, 1), :]
    for k in range(n_st):
        acc = acc + _state_row(st_ref, k) * w_ref[pl.ds(k, 1), :]
    y = _ln_rows(acc + bdw_ref[...], lng_ref[...], lnb_ref[...])
    o_ref[...] = (y * jax.nn.sigmoid(y)).astype(BF16)
    _push_state(st_ref, g_ref[...], ns_ref, n_st)


def _conv_a_sample(state, layer, g, w_dw, b_dw, ln_g, ln_b, tb):
    n = g.shape[0]
    n_st = CONV_A_WIDTH - 1
    vspec = pl.BlockSpec((1, D_MODEL), lambda i: (0, 0))
    out, ns = pl.pallas_call(
        _conv_a_sample_kernel,
        grid=(n // tb,),
        in_specs=[pl.BlockSpec((None, tb, n_st * D_MODEL), lambda i: (layer, i, 0)),
                  pl.BlockSpec((tb, D_MODEL), lambda i: (i, 0)),
                  pl.BlockSpec((32, D_MODEL), lambda i: (0, 0)),
                  vspec, vspec, vspec],
        out_specs=[pl.BlockSpec((tb, D_MODEL), lambda i: (i, 0)),
                   pl.BlockSpec((tb, n_st * D_MODEL), lambda i: (i, 0))],
        out_shape=[jax.ShapeDtypeStruct((n, D_MODEL), BF16),
                   jax.ShapeDtypeStruct((n, n_st * D_MODEL), F32)],
        compiler_params=_params("arbitrary"),
        name="conv_a_sample",
    )(_flat_state(state), g, _pad_rows(w_dw, 32), _vec(b_dw), _vec(ln_g), _vec(ln_b))
    return out, ns.reshape(1, n, n_st, D_MODEL)


POOL_HALO = 16


def _pool_prompt_kernel(x_ref, g_ref, w_ref, sc_ref, o_ref, hs_ref, ext_ref, diff_ref, wb_ref,
                        *, tt):
    t = pl.program_id(1)
    H = POOL_HALO
    _carry_halo(ext_ref, t, tt, H)

    rc = 32

    def norm(r0):
        ext_ref[pl.ds(H + r0, rc), :] = _rms_rows(x_ref[0, pl.ds(r0, rc), :], g_ref[...])
    _for_chunks(tt, rc, norm, unroll=2)

    hs_ref[0, 0] = ext_ref[pl.ds(tt, H), :]

    @pl.when((pl.program_id(0) == 0) & (t == 0))
    def _():
        wb_ref[...] = w_ref[...].astype(BF16)

    pr = 64
    win = pr + H
    for gi, w in enumerate(POOL_WINDOWS):
        lanes = pl.ds(gi * POOL_GROUP, POOL_GROUP)

        def rows(r0, w=w, lanes=lanes):
            s = ext_ref[pl.ds(r0, win), lanes]
            cur = s[H:]
            d = 1
            while d < w:
                s = s + pltpu.roll(s, d, 0)
                d *= 2
            pos = t * tt + r0 + lax.broadcasted_iota(jnp.int32, (pr, 1), 0)
            cnt = jnp.minimum(pos + 1, w).astype(F32)
            diff_ref[pl.ds(r0, pr), lanes] = (s[H:] / cnt - cur).astype(BF16)
        _for_chunks(tt, pr, rows)

    for gi in range(len(POOL_WINDOWS)):
        sl = slice(gi * POOL_GROUP, (gi + 1) * POOL_GROUP)
        y = jnp.dot(diff_ref[:, sl], wb_ref[gi], preferred_element_type=F32)
        o_ref[0, :, sl] = x_ref[0, :, sl] + y * sc_ref[:, sl]


def _pool_prompt(x3, g, w_grp, layer, scale):
    b, l, _ = x3.shape
    nt = l // TT
    ng = len(POOL_WINDOWS)
    vspec = pl.BlockSpec((1, D_MODEL), lambda i, t: (0, 0))
    return pl.pallas_call(
        functools.partial(_pool_prompt_kernel, tt=TT),
        grid=(b, nt),
        in_specs=[pl.BlockSpec((1, TT, D_MODEL), lambda i, t: (i, t, 0)),
                  vspec,
                  pl.BlockSpec((None, ng, POOL_GROUP, POOL_GROUP), lambda i, t: (layer, 0, 0, 0)),
                  vspec],
        out_specs=[pl.BlockSpec((1, TT, D_MODEL), lambda i, t: (i, t, 0)),
                   pl.BlockSpec((1, 1, POOL_HALO, D_MODEL), lambda i, t: (i, t, 0, 0))],
        out_shape=[jax.ShapeDtypeStruct((b, l, D_MODEL), F32),
                   jax.ShapeDtypeStruct((b, nt, POOL_HALO, D_MODEL), F32)],
        scratch_shapes=[pltpu.VMEM((TT + POOL_HALO, D_MODEL), F32),
                        pltpu.VMEM((TT, D_MODEL), BF16),
                        pltpu.VMEM((ng, POOL_GROUP, POOL_GROUP), BF16)],
        compiler_params=_params("arbitrary", "arbitrary"),
        name="pool_prompt",
    )(x3, _vec(g), w_grp, _vec(scale))


def _pool_sample_kernel(x_ref, st_ref, g_ref, w_ref, sc_ref, o_ref, ns_ref):
    x = x_ref[...]
    h = _rms_rows(x, g_ref[...])
    for gi, w in enumerate(POOL_WINDOWS):
        sl = slice(gi * POOL_GROUP, (gi + 1) * POOL_GROUP)
        cur = h[:, sl]
        s = cur
        for d in range(1, w):
            s = s + st_ref[:, pl.ds((POOL_STATE - d) * D_MODEL + gi * POOL_GROUP, POOL_GROUP)]
        diff = s / float(min(PAST_LEN + 1, w)) - cur
        y = jnp.dot(diff.astype(BF16), w_ref[gi].astype(BF16), preferred_element_type=F32)
        o_ref[:, sl] = x[:, sl] + y * sc_ref[:, sl]
    _push_state(st_ref, h, ns_ref, POOL_STATE)


def _pool_sample(x, state, g, w_grp, layer, scale, tb):
    n = x.shape[0]
    ng = len(POOL_WINDOWS)
    vspec = pl.BlockSpec((1, D_MODEL), lambda i: (0, 0))
    row = pl.BlockSpec((tb, D_MODEL), lambda i: (i, 0))
    out, ns = pl.pallas_call(
        _pool_sample_kernel,
        grid=(n // tb,),
        in_specs=[row,
                  pl.BlockSpec((None, tb, POOL_STATE * D_MODEL), lambda i: (layer, i, 0)),
                  vspec,
                  pl.BlockSpec((None, ng, POOL_GROUP, POOL_GROUP), lambda i: (layer, 0, 0, 0)),
                  vspec],
        out_specs=[row, pl.BlockSpec((tb, POOL_STATE * D_MODEL), lambda i: (i, 0))],
        out_shape=[jax.ShapeDtypeStruct((n, D_MODEL), F32),
                   jax.ShapeDtypeStruct((n, POOL_STATE * D_MODEL), F32)],
        compiler_params=_params("arbitrary"),
        name="pool_sample",
    )(x, _flat_state(state), _vec(g), w_grp, _vec(scale))
    return out, ns.reshape(1, n, POOL_STATE, D_MODEL)


C_ROWS = 128


def _c_in_kernel(xp_ref, xs_ref, g_ref, wb_ref, wc_ref, wx_ref, wconv_ref,
                 z_ref, tail_ref, bgs_ref, cxs_ref, xnp_ref, xns_ref, ext_ref, carry_ref,
                 *, blocks_per_seq):
    i, j = pl.program_id(0), pl.program_id(1)
    last_i = pl.num_programs(0) - 1
    H = SUBLANES
    tm = xp_ref.shape[0]

    @pl.when(j == 0)
    def _():
        _rms_to_bf16(xp_ref, g_ref, xnp_ref)

    @pl.when((j == 0) & (i == last_i))
    def _():
        _rms_to_bf16(xs_ref, g_ref, xns_ref)

    def parts(xn_ref):
        xn = xn_ref[...]
        return tuple(jnp.dot(xn, w_ref[...], preferred_element_type=F32)
                     for w_ref in (wb_ref, wc_ref, wx_ref))

    bg, c, xv = parts(xnp_ref)
    cx = c * xv
    prev = carry_ref[j]
    ext_ref[pl.ds(0, H), :] = jnp.where(i % blocks_per_seq == 0, jnp.zeros_like(prev), prev)
    ext_ref[pl.ds(H, tm), :] = cx
    carry_ref[j] = cx[tm - H:]
    tail_ref[0] = cx[tm - H:]
    win = C_ROWS + H
    for r0 in range(0, tm, C_ROWS):
        w = ext_ref[pl.ds(r0, win), :]
        y = (pltpu.roll(w, 2, 0)[H:] * wconv_ref[pl.ds(0, 1), :]
             + pltpu.roll(w, 1, 0)[H:] * wconv_ref[pl.ds(1, 1), :]
             + w[H:] * wconv_ref[pl.ds(2, 1), :])
        z_ref[pl.ds(r0, C_ROWS), :] = (bg[r0:r0 + C_ROWS] * y).astype(BF16)

    @pl.when(i == last_i)
    def _():
        bs, cs, xvs = parts(xns_ref)
        bgs_ref[...] = bs
        cxs_ref[...] = cs * xvs


def _c_in_conv(xp, xs, g, w, w_conv, seq_len):
    mp, ms = xp.shape[0], xs.shape[0]
    tn = w.shape[2]
    nj = D_MODEL // tn
    ni = mp // TM
    assert seq_len % TM == 0 and w.shape == (3 * nj, D_MODEL, tn)
    wspecs = [pl.BlockSpec((None, D_MODEL, tn), lambda i, j, p=p: (p * nj + j, 0, 0))
              for p in range(3)]
    tile_s = pl.BlockSpec((ms, tn), lambda i, j: (0, _sample_col(i, j, ni)))
    return pl.pallas_call(
        functools.partial(_c_in_kernel, blocks_per_seq=seq_len // TM),
        grid=(ni, nj),
        in_specs=[pl.BlockSpec((TM, D_MODEL), lambda i, j: (i, 0)),
                  pl.BlockSpec((ms, D_MODEL), lambda i, j: (0, 0)),
                  pl.BlockSpec((1, D_MODEL), lambda i, j: (0, 0))] + wspecs
                 + [pl.BlockSpec((SUBLANES, tn), lambda i, j: (0, j))],
        out_specs=[pl.BlockSpec((TM, tn), lambda i, j: (i, j)),
                   pl.BlockSpec((1, SUBLANES, tn), lambda i, j: (i, 0, j)),
                   tile_s, tile_s],
        out_shape=[jax.ShapeDtypeStruct((mp, D_MODEL), BF16),
                   jax.ShapeDtypeStruct((ni, SUBLANES, D_MODEL), F32),
                   jax.ShapeDtypeStruct((ms, D_MODEL), F32),
                   jax.ShapeDtypeStruct((ms, D_MODEL), F32)],
        scratch_shapes=[pltpu.VMEM((TM, D_MODEL), BF16), pltpu.VMEM((ms, D_MODEL), BF16),
                        pltpu.VMEM((TM + SUBLANES, tn), F32),
                        pltpu.VMEM((nj, SUBLANES, tn), F32)],
        compiler_params=_params("arbitrary", "arbitrary"),
        name="c_in_conv",
    )(xp, xs, _vec(g), w, w, w, _pad_rows(w_conv, SUBLANES))


def _conv_c_sample_kernel(b_ref, cx_ref, st_ref, w_ref, o_ref, ns_ref):
    y = (_state_row(st_ref, 0) * w_ref[pl.ds(0, 1), :] + _state_row(st_ref, 1) * w_ref[pl.ds(1, 1), :]
         + cx_ref[...] * w_ref[pl.ds(2, 1), :])
    o_ref[...] = (b_ref[...] * y).astype(BF16)
    _push_state(st_ref, cx_ref[...], ns_ref, CONV_C_WIDTH - 1)


def _conv_c_sample(bg, cx, state, layer, w_conv):
    n = bg.shape[0]
    n_st = CONV_C_WIDTH - 1
    row = pl.BlockSpec((n, D_MODEL), lambda i: (0, 0))
    out, ns = pl.pallas_call(
        _conv_c_sample_kernel,
        grid=(1,),
        in_specs=[row, row,
                  pl.BlockSpec((None, n, n_st * D_MODEL), lambda i: (layer, 0, 0)),
                  pl.BlockSpec((SUBLANES, D_MODEL), lambda i: (0, 0))],
        out_specs=[row, pl.BlockSpec((n, n_st * D_MODEL), lambda i: (0, 0))],
        out_shape=[jax.ShapeDtypeStruct((n, D_MODEL), BF16),
                   jax.ShapeDtypeStruct((n, n_st * D_MODEL), F32)],
        compiler_params=_params("arbitrary"),
        name="conv_c_sample",
    )(bg, cx, _flat_state(state), _pad_rows(w_conv, SUBLANES))
    return out, ns.reshape(1, n, n_st, D_MODEL)


def _sg_kernel(up_ref, vp_ref, us_ref, vs_ref, lng_ref, lnb_ref, ws_ref, bs_ref, ws0_ref, bs0_ref,
               wo_ref, xp_ref, xs_ref, op_ref, os_ref, vns_ref, vn_ref, gt_ref, wsm_ref):
    i = pl.program_id(0)
    tm = up_ref.shape[0]

    @pl.when(i == 0)
    def _():
        row = lax.broadcasted_iota(jnp.int32, (CHUNK, CHUNK), 0)
        col = lax.broadcasted_iota(jnp.int32, (CHUNK, CHUNK), 1)
        for hd in range(N_SG_HEADS):
            wsm_ref[hd] = jnp.where(col <= row, ws_ref[hd], 0.0).astype(BF16)

    rc = 32

    def norm(r0):
        vn_ref[pl.ds(r0, rc), :] = _ln_rows(vp_ref[pl.ds(r0, rc), :], lng_ref[...],
                                            lnb_ref[...]).astype(BF16)
    _for_chunks(tm, rc, norm, unroll=2)

    for r0 in range(0, tm, CHUNK):
        for hd in range(N_SG_HEADS):
            lanes = pl.ds(hd * SG_HEAD_DIM, SG_HEAD_DIM)
            mixed = jnp.dot(wsm_ref[hd], vn_ref[pl.ds(r0, CHUNK), lanes],
                            preferred_element_type=F32) + bs_ref[:, lanes]
            gt_ref[pl.ds(r0, CHUNK), lanes] = (up_ref[pl.ds(r0, CHUNK), lanes].astype(F32)
                                               * mixed).astype(BF16)

    op_ref[...] = xp_ref[...] + jnp.dot(gt_ref[...], wo_ref[...], preferred_element_type=F32)

    @pl.when(i == pl.num_programs(0) - 1)
    def _():
        vn = _ln_rows(vs_ref[...], lng_ref[...], lnb_ref[...])
        vns_ref[...] = vn
        mixed = ws0_ref[...] * vn + bs0_ref[...]
        gated = (us_ref[...].astype(F32) * mixed).astype(BF16)
        os_ref[...] = xs_ref[...] + jnp.dot(gated, wo_ref[...], preferred_element_type=F32)


def _sg(up, vp, us, vs, ln_g, ln_b, w_s, layer, bs_map, ws0_map, bs0_map, w_o, xp, xs):
    mp, ms = up.shape[0], us.shape[0]
    vspec = pl.BlockSpec((1, D_MODEL), lambda i: (0, 0))
    rowp = pl.BlockSpec((TM_FULL, D_MODEL), lambda i: (i, 0))
    rows = pl.BlockSpec((ms, D_MODEL), lambda i: (0, 0))
    return pl.pallas_call(
        _sg_kernel,
        grid=(mp // TM_FULL,),
        in_specs=[rowp, rowp, rows, rows, vspec, vspec,
                  pl.BlockSpec((None, N_SG_HEADS, CHUNK, CHUNK), lambda i: (layer, 0, 0, 0)),
                  pl.BlockSpec((CHUNK, D_MODEL), lambda i: (0, 0)),
                  vspec, vspec,
                  pl.BlockSpec((D_MODEL, D_MODEL), lambda i: (0, 0), pipeline_mode=pl.Buffered(1)),
                  rowp, rows],
        out_specs=[rowp, rows, rows],
        out_shape=[jax.ShapeDtypeStruct((mp, D_MODEL), F32),
                   jax.ShapeDtypeStruct((ms, D_MODEL), F32),
                   jax.ShapeDtypeStruct((ms, D_MODEL), F32)],
        scratch_shapes=[pltpu.VMEM((TM_FULL, D_MODEL), BF16), pltpu.VMEM((TM_FULL, D_MODEL), BF16),
                        pltpu.VMEM((N_SG_HEADS, CHUNK, CHUNK), BF16)],
        compiler_params=_params("arbitrary"),
        name="sg_mix",
    )(up, vp, us, vs, _vec(ln_g), _vec(ln_b), w_s, bs_map, ws0_map, bs0_map, w_o, xp, xs)


def kernel(x_prompt, x_sample, state_conv_a, state_pool, state_short_conv, norm_mix, norm_mlp, norm_final, a_w_pw1, a_b_pw1, a_w_dw, a_b_dw, a_ln_g, a_ln_b, a_w_pw2, b_w_grp, b_scale, c_w_in, c_w_conv, c_w_out, d_w_uv, d_ln_g, d_ln_b, d_w_s, d_b_s, d_w_o, mlp_w1, mlp_w2):
    b, l, _ = x_prompt.shape
    mp = b * l
    ms = x_sample.shape[0]
    assert x_sample.shape[1] == 1 and mp % TM == 0 and l % TT == 0
    flat3 = lambda a: a.reshape(b, l, D_MODEL)
    xp = x_prompt.reshape(mp, D_MODEL)
    xs = x_sample.reshape(ms, D_MODEL)
    tf = 512
    tn_a, tn_c, tn_d = 512, 512, 1024

    w_pw1 = _tile_major(a_w_pw1[0].astype(BF16), tn_a)
    (gp,), (gs,), (w_pw2,) = _mm_parts(
        "a_pw1_glu", xp, xs, norm_mix[0], w_pw1, a_b_pw1[0], 2, (F32,), _glu_epilogue,
        cast_jobs=[(a_w_pw2, 0, None)])
    cp3, (w1_0, w2_0) = _conv_a_prompt(flat3(gp), a_w_dw[0], a_b_dw[0], a_ln_g[0], a_ln_b[0],
                                       cast_jobs=[(mlp_w1, 0, tf), (mlp_w2, 0, None)])
    cp = cp3.reshape(mp, D_MODEL)
    cs, sa = _conv_a_sample(state_conv_a, 0, gs, a_w_dw[0], a_b_dw[0], a_ln_g[0], a_ln_b[0], 32)
    pa = flat3(gp)[:, l - (CONV_A_WIDTH - 1):]
    xp, xs = _mm_res("a_pw2", cp, cs, w_pw2, xp, xs)
    xp, xs, (w1_1, w2_1, w_in, w_out) = _mlp(
        "mlp0", xp, xs, norm_mlp[0], w1_0, w2_0,
        cast_jobs=[(mlp_w1, 1, tf), (mlp_w2, 1, None), (c_w_in, 0, tn_c), (c_w_out, 0, None)])

    xp3, hs = _pool_prompt(flat3(xp), norm_mix[1], b_w_grp, 0, b_scale[0])
    xp = xp3.reshape(mp, D_MODEL)
    pb = hs[:, -1, POOL_HALO - POOL_STATE:]
    xs, sb = _pool_sample(xs, state_pool, norm_mix[1], b_w_grp, 0, b_scale[0], 32)
    xp, xs, (w1_2, w2_2, w_uv, w_o) = _mlp(
        "mlp1", xp, xs, norm_mlp[1], w1_1, w2_1,
        cast_jobs=[(mlp_w1, 2, tf), (mlp_w2, 2, None), (d_w_uv, 0, tn_d), (d_w_o, 0, None)])

    zp, cx_tail, bgs, cxs = _c_in_conv(xp, xs, norm_mix[2], w_in, c_w_conv[0], l)
    zs, sc = _conv_c_sample(bgs, cxs, state_short_conv, 0, c_w_conv[0])
    pc = cx_tail.reshape(b, l // TM, SUBLANES, D_MODEL)[:, -1, SUBLANES - (CONV_C_WIDTH - 1):]
    xp, xs = _mm_res("c_out", zp, zs, w_out, xp, xs)
    xp, xs, (w1_3, w2_3) = _mlp("mlp2", xp, xs, norm_mlp[2], w1_2, w2_2,
                                cast_jobs=[(mlp_w1, 3, tf), (mlp_w2, 3, None)])

    (up, vp), (us, vs), _ = _mm_parts("d_uv_gelu", xp, xs, norm_mix[3], w_uv, None, 2,
                                      (BF16, F32), _gelu_epilogue)
    bs_map = jnp.repeat(d_b_s[0].T, SG_HEAD_DIM, axis=1)
    ws0_map = _vec(jnp.repeat(d_w_s[0, :, 0, 0], SG_HEAD_DIM))
    bs0_map = _vec(jnp.repeat(d_b_s[0, :, 0], SG_HEAD_DIM))
    xp, xs, sd = _sg(up, vp, us, vs, d_ln_g[0], d_ln_b[0], d_w_s, 0, bs_map, ws0_map, bs0_map,
                     w_o, xp, xs)
    yp, ys, _ = _mlp("mlp3", xp, xs, norm_mlp[3], w1_3, w2_3, final_g=norm_final)

    return (yp.reshape(b, l, D_MODEL), ys.reshape(ms, 1, D_MODEL),
            pa[None], sa, pb[None], sb, pc[None], sc,
            sd.reshape(1, ms, 1, D_MODEL))
```

```python
import functools

import jax
import jax.numpy as jnp
from jax import lax
from jax.experimental import pallas as pl
from jax.experimental.pallas import tpu as pltpu

D_MODEL = 2048
D_FF = 4 * D_MODEL
CONV_A_WIDTH = 31
POOL_WINDOWS = (2, 4, 8, 16)
POOL_GROUP = D_MODEL // len(POOL_WINDOWS)
POOL_STATE = max(POOL_WINDOWS) - 1
CONV_C_WIDTH = 3
CHUNK = 128
N_SG_HEADS = 8
SG_HEAD_DIM = D_MODEL // N_SG_HEADS
PAST_LEN = 16384
RMS_EPS = 1e-6
LN_EPS = 1e-5

SUBLANES = 8
BF16_ROWS = 16
VMEM_LIMIT_BYTES = 56 * 1024 * 1024
HALO = 32

TM = 1024
TM_FULL = 512
TT = 256
NORM_ROWS = 32
NORM_UNROLL = 4

F32 = jnp.float32
BF16 = jnp.bfloat16


def _params(*sem):
    return pltpu.CompilerParams(dimension_semantics=sem, vmem_limit_bytes=VMEM_LIMIT_BYTES)


def _for_chunks(n, chunk, body, unroll=1):
    assert n % chunk == 0
    if n == chunk:
        body(0)
    else:
        def step(i, c):
            body(pl.multiple_of(i * chunk, chunk))
            return c
        lax.fori_loop(0, n // chunk, step, 0, unroll=unroll)


def _rms_rows(x, g):
    r = lax.rsqrt(jnp.mean(x * x, axis=-1, keepdims=True) + RMS_EPS)
    return x * r * g


def _ln_rows(x, g, b):
    mu = jnp.mean(x, axis=-1, keepdims=True)
    xc = x - mu
    var = jnp.mean(xc * xc, axis=-1, keepdims=True)
    return xc * lax.rsqrt(var + LN_EPS) * g + b


def _row_chunk(tm):
    for c in (64, 32, 16, 8):
        if tm % c == 0:
            return c
    raise ValueError(tm)


def _rms_to_bf16(x_ref, g_ref, xn_ref, copy_ref=None):
    n = x_ref.shape[0]
    rc = _row_chunk(n)

    def body(r0):
        x = x_ref[pl.ds(r0, rc), :]
        xn_ref[pl.ds(r0, rc), :] = _rms_rows(x, g_ref[...]).astype(BF16)
        if copy_ref is not None:
            copy_ref[pl.ds(r0, rc), :] = x
    _for_chunks(n, rc, body, unroll=2)


def _vec(v):
    return v.reshape(1, D_MODEL)


def _sample_col(i, j, ni):
    return jnp.where(i == ni - 1, j, 0)


def _wspec(block, index, layer, **kw):
    if layer is None:
        return pl.BlockSpec(block, index, **kw)
    return pl.BlockSpec((None,) + block, lambda *g: (layer,) + tuple(index(*g)), **kw)


def _cast_job_specs(jobs, n_steps, step_of):
    in_specs, args, out_specs, out_shapes = [], [], [], []
    for src, layer in jobs:
        _, r, c = src.shape
        assert r % (n_steps * BF16_ROWS) == 0, (src.shape, n_steps)
        pr = r // n_steps
        in_specs.append(pl.BlockSpec((None, pr, c),
                                     lambda *g, layer=layer: (layer, step_of(*g), 0)))
        args.append(src)
        out_specs.append(pl.BlockSpec((pr, c), lambda *g: (step_of(*g), 0)))
        out_shapes.append(jax.ShapeDtypeStruct((r, c), BF16))
    return in_specs, args, out_specs, out_shapes


def _run_cast_jobs(src_refs, dst_refs):
    for s_ref, d_ref in zip(src_refs, dst_refs):
        d_ref[...] = s_ref[...].astype(BF16)


def _mm_parts_kernel(*refs, n_parts, n_out, has_bias, n_jobs, epilogue):
    xp_ref, xs_ref, g_ref = refs[:3]
    w_refs = refs[3:3 + n_parts]
    pos = 3 + n_parts
    b_refs = refs[pos:pos + n_parts] if has_bias else ()
    pos += n_parts if has_bias else 0
    job_src = refs[pos:pos + n_jobs]
    pos += n_jobs
    op_refs = refs[pos:pos + n_out]
    os_refs = refs[pos + n_out:pos + 2 * n_out]
    pos += 2 * n_out
    job_dst = refs[pos:pos + n_jobs]
    xnp_ref, xns_ref = refs[pos + n_jobs:]
    i, j = pl.program_id(0), pl.program_id(1)
    last_i = pl.num_programs(0) - 1

    @pl.when(j == 0)
    def _():
        _rms_to_bf16(xp_ref, g_ref, xnp_ref)

    @pl.when((j == 0) & (i == last_i))
    def _():
        _rms_to_bf16(xs_ref, g_ref, xns_ref)

    def rows(xn_ref, o_refs):
        parts = []
        for p in range(n_parts):
            a = jnp.dot(xn_ref[...], w_refs[p][...], preferred_element_type=F32)
            if has_bias:
                a = a + b_refs[p][...]
            parts.append(a)
        for o_ref, o in zip(o_refs, epilogue(*parts)):
            o_ref[...] = o.astype(o_ref.dtype)

    rows(xnp_ref, op_refs)
    _run_cast_jobs(job_src, job_dst)

    @pl.when(i == last_i)
    def _():
        rows(xns_ref, os_refs)


def _mm_parts(name, xp, xs, g, w, layer, bias, n_parts, out_dtypes, epilogue, tn, cast_jobs=()):
    mp, ms = xp.shape[0], xs.shape[0]
    nj = D_MODEL // tn
    ni = mp // TM
    in_specs = [pl.BlockSpec((TM, D_MODEL), lambda i, j: (i, 0)),
                pl.BlockSpec((ms, D_MODEL), lambda i, j: (0, 0)),
                pl.BlockSpec((1, D_MODEL), lambda i, j: (0, 0))]
    args = [xp, xs, _vec(g)]
    for p in range(n_parts):
        in_specs.append(_wspec((D_MODEL, tn), lambda i, j, p=p: (0, p * nj + j), layer))
        args.append(w)
    if bias is not None:
        for p in range(n_parts):
            in_specs.append(pl.BlockSpec((1, tn), lambda i, j, p=p: (0, p * nj + j)))
            args.append(bias.reshape(1, n_parts * D_MODEL))
    j_in, j_args, j_out, j_shapes = _cast_job_specs(cast_jobs, ni * nj, lambda i, j: i * nj + j)
    kern = functools.partial(_mm_parts_kernel, n_parts=n_parts, n_out=len(out_dtypes),
                             has_bias=bias is not None, n_jobs=len(cast_jobs), epilogue=epilogue)
    outs = pl.pallas_call(
        kern,
        grid=(ni, nj),
        in_specs=in_specs + j_in,
        out_specs=([pl.BlockSpec((TM, tn), lambda i, j: (i, j)) for _ in out_dtypes]
                   + [pl.BlockSpec((ms, tn), lambda i, j: (0, _sample_col(i, j, ni)))
                      for _ in out_dtypes] + j_out),
        out_shape=([jax.ShapeDtypeStruct((mp, D_MODEL), dt) for dt in out_dtypes]
                   + [jax.ShapeDtypeStruct((ms, D_MODEL), dt) for dt in out_dtypes] + j_shapes),
        scratch_shapes=[pltpu.VMEM((TM, D_MODEL), BF16), pltpu.VMEM((ms, D_MODEL), BF16)],
        compiler_params=_params("arbitrary", "arbitrary"),
        name=name,
    )(*args, *j_args)
    n = len(out_dtypes)
    return outs[:n], outs[n:2 * n], outs[2 * n:]


def _glu_epilogue(a, b):
    return (a * jax.nn.sigmoid(b),)


def _gelu_epilogue(a, b):
    return (jax.nn.gelu(a), jax.nn.gelu(b))


def _mm_res_kernel(ap_ref, as_ref, w_ref, xp_ref, xs_ref, op_ref, os_ref):
    op_ref[...] = xp_ref[...] + jnp.dot(ap_ref[...], w_ref[...], preferred_element_type=F32)

    @pl.when(pl.program_id(0) == pl.num_programs(0) - 1)
    def _():
        os_ref[...] = xs_ref[...] + jnp.dot(as_ref[...], w_ref[...], preferred_element_type=F32)


def _mm_res(name, ap, as_, w, xp, xs):
    mp, ms = ap.shape[0], as_.shape[0]
    rowp = pl.BlockSpec((TM_FULL, D_MODEL), lambda i: (i, 0))
    rows = pl.BlockSpec((ms, D_MODEL), lambda i: (0, 0))
    return pl.pallas_call(
        _mm_res_kernel,
        grid=(mp // TM_FULL,),
        in_specs=[rowp, rows,
                  pl.BlockSpec((D_MODEL, D_MODEL), lambda i: (0, 0), pipeline_mode=pl.Buffered(1)),
                  rowp, rows],
        out_specs=[rowp, rows],
        out_shape=[jax.ShapeDtypeStruct((mp, D_MODEL), F32),
                   jax.ShapeDtypeStruct((ms, D_MODEL), F32)],
        compiler_params=_params("arbitrary"),
        name=name,
    )(ap, as_, w, xp, xs)


def _mlp_kernel(*refs, final, n_jobs):
    n_in = 6 if final else 5
    xp_ref, xs_ref, g_ref, w1_ref, w2_ref = refs[:5]
    gf_ref = refs[5] if final else None
    job_src = refs[n_in:n_in + n_jobs]
    op_ref, os_ref = refs[n_in + n_jobs:n_in + n_jobs + 2]
    job_dst = refs[n_in + n_jobs + 2:n_in + 2 * n_jobs + 2]
    xnp_ref, xns_ref = refs[n_in + 2 * n_jobs + 2:]
    i, j = pl.program_id(0), pl.program_id(1)
    last_i = pl.num_programs(0) - 1
    last_j = pl.num_programs(1) - 1

    @pl.when(j == 0)
    def _():
        _rms_to_bf16(xp_ref, g_ref, xnp_ref, copy_ref=op_ref)

    @pl.when((j == 0) & (i == last_i))
    def _():
        _rms_to_bf16(xs_ref, g_ref, xns_ref, copy_ref=os_ref)

    def rows(xn_ref, o_ref):
        h = jnp.dot(xn_ref[...], w1_ref[...], preferred_element_type=F32)
        h = jnp.square(jnp.maximum(h, 0.0)).astype(BF16)
        o_ref[...] += jnp.dot(h, w2_ref[...], preferred_element_type=F32)

    rows(xnp_ref, op_ref)
    _run_cast_jobs(job_src, job_dst)

    @pl.when(i == last_i)
    def _():
        rows(xns_ref, os_ref)

    if final:
        def norm(o_ref):
            n = o_ref.shape[0]
            rc = _row_chunk(n)

            def body(r0):
                o_ref[pl.ds(r0, rc), :] = _rms_rows(o_ref[pl.ds(r0, rc), :], gf_ref[...])
            _for_chunks(n, rc, body, unroll=2)

        @pl.when(j == last_j)
        def _():
            norm(op_ref)

        @pl.when((j == last_j) & (i == last_i))
        def _():
            norm(os_ref)


def _mlp(name, xp, xs, g, w1, w2, layer, tf, final_g=None, cast_jobs=()):
    mp, ms = xp.shape[0], xs.shape[0]
    final = final_g is not None
    ni, nj = mp // TM, D_FF // tf
    in_specs = [pl.BlockSpec((TM, D_MODEL), lambda i, j: (i, 0)),
                pl.BlockSpec((ms, D_MODEL), lambda i, j: (0, 0)),
                pl.BlockSpec((1, D_MODEL), lambda i, j: (0, 0)),
                _wspec((D_MODEL, tf), lambda i, j: (0, j), layer),
                _wspec((tf, D_MODEL), lambda i, j: (j, 0), layer)]
    args = [xp, xs, _vec(g), w1, w2]
    if final:
        in_specs.append(pl.BlockSpec((1, D_MODEL), lambda i, j: (0, 0)))
        args.append(_vec(final_g))
    j_in, j_args, j_out, j_shapes = _cast_job_specs(cast_jobs, ni * nj, lambda i, j: i * nj + j)
    outs = pl.pallas_call(
        functools.partial(_mlp_kernel, final=final, n_jobs=len(cast_jobs)),
        grid=(ni, nj),
        in_specs=in_specs + j_in,
        out_specs=[pl.BlockSpec((TM, D_MODEL), lambda i, j: (i, 0)),
                   pl.BlockSpec((ms, D_MODEL), lambda i, j: (0, 0))] + j_out,
        out_shape=[jax.ShapeDtypeStruct((mp, D_MODEL), F32),
                   jax.ShapeDtypeStruct((ms, D_MODEL), F32)] + j_shapes,
        scratch_shapes=[pltpu.VMEM((TM, D_MODEL), BF16), pltpu.VMEM((ms, D_MODEL), BF16)],
        compiler_params=_params("arbitrary", "arbitrary"),
        name=name,
    )(*args, *j_args)
    return outs[0], outs[1], outs[2:]


CONV_ROWS = 128
CONV_LANES = 256


def _carry_halo(ext_ref, t, tt, halo):
    @pl.when(t == 0)
    def _():
        ext_ref[pl.ds(0, halo), :] = jnp.zeros((halo, D_MODEL), F32)

    @pl.when(t > 0)
    def _():
        ext_ref[pl.ds(0, halo), :] = ext_ref[pl.ds(tt, halo), :]


def _conv_a_kernel(*refs, tt, n_jobs):
    g_ref, w_ref, bdw_ref, lng_ref, lnb_ref = refs[:5]
    job_src = refs[5:5 + n_jobs]
    o_ref = refs[5 + n_jobs]
    job_dst = refs[6 + n_jobs:6 + 2 * n_jobs]
    ext_ref, c_ref = refs[6 + 2 * n_jobs:]
    _run_cast_jobs(job_src, job_dst)
    _carry_halo(ext_ref, pl.program_id(1), tt, HALO)

    def copy_in(r0):
        ext_ref[pl.ds(HALO + r0, 64), :] = g_ref[0, pl.ds(r0, 64), :]
    _for_chunks(tt, 64, copy_in)

    off = HALO - (CONV_A_WIDTH - 1)
    win = CONV_ROWS + HALO
    tiles = CONV_ROWS // SUBLANES
    for c0 in range(0, D_MODEL, CONV_LANES):
        lanes = pl.ds(c0, CONV_LANES)

        def rows(r0, lanes=lanes):
            w = ext_ref[pl.ds(r0, win), lanes]
            acc = jnp.zeros((tiles, SUBLANES, CONV_LANES), F32)
            for s in range(SUBLANES):
                ws = w if s == 0 else pltpu.roll(w, win - s, 0)
                for k in range(CONV_A_WIDTH):
                    if (off + k) % SUBLANES == s:
                        a = off + k - s
                        tap = ws[a:a + CONV_ROWS].reshape(tiles, SUBLANES, CONV_LANES)
                        acc = acc + tap * w_ref[k, :, lanes][None]
            c_ref[pl.ds(r0, CONV_ROWS), lanes] = acc.reshape(CONV_ROWS, CONV_LANES)
        _for_chunks(tt, CONV_ROWS, rows)

    def norm(r0):
        c = c_ref[pl.ds(r0, NORM_ROWS), :] + bdw_ref[...]
        y = _ln_rows(c, lng_ref[...], lnb_ref[...])
        o_ref[0, pl.ds(r0, NORM_ROWS), :] = (y * jax.nn.sigmoid(y)).astype(BF16)
    _for_chunks(tt, NORM_ROWS, norm, unroll=NORM_UNROLL)


def _pad_rows(w, rows):
    return jnp.zeros((rows, D_MODEL), F32).at[:w.shape[0]].set(w)


def _conv_a_prompt(g3, w_dw, b_dw, ln_g, ln_b, cast_jobs=()):
    b, l, _ = g3.shape
    nt = l // TT
    vspec = pl.BlockSpec((1, D_MODEL), lambda i, t: (0, 0))
    w_rep = jnp.broadcast_to(_pad_rows(w_dw, 32)[:, None, :], (32, SUBLANES, D_MODEL))
    j_in, j_args, j_out, j_shapes = _cast_job_specs(cast_jobs, b * nt, lambda i, t: i * nt + t)
    outs = pl.pallas_call(
        functools.partial(_conv_a_kernel, tt=TT, n_jobs=len(cast_jobs)),
        grid=(b, nt),
        in_specs=[pl.BlockSpec((1, TT, D_MODEL), lambda i, t: (i, t, 0)),
                  pl.BlockSpec((32, SUBLANES, D_MODEL), lambda i, t: (0, 0, 0)),
                  vspec, vspec, vspec] + j_in,
        out_specs=[pl.BlockSpec((1, TT, D_MODEL), lambda i, t: (i, t, 0))] + j_out,
        out_shape=[jax.ShapeDtypeStruct((b, l, D_MODEL), BF16)] + j_shapes,
        scratch_shapes=[pltpu.VMEM((TT + HALO, D_MODEL), F32),
                        pltpu.VMEM((TT, D_MODEL), F32)],
        compiler_params=_params("arbitrary", "arbitrary"),
        name="conv_a_prompt",
    )(g3, w_rep, _vec(b_dw), _vec(ln_g), _vec(ln_b), *j_args)
    return outs[0], outs[1:]


def _rows_major(state):
    return jnp.transpose(state, (0, 2, 1, 3))


def _push_state(st_ref, new_row, ns_ref):
    n_st = st_ref.shape[0]
    for k in range(n_st - 1):
        ns_ref[k] = st_ref[k + 1]
    ns_ref[n_st - 1] = new_row


def _state_specs(n_st, tb, layer):
    return (pl.BlockSpec((None, n_st, tb, D_MODEL), lambda i: (layer, 0, i, 0)),
            pl.BlockSpec((None, n_st, tb, D_MODEL), lambda i: (0, 0, i, 0)))


def _conv_a_sample_kernel(st_ref, g_ref, w_ref, bdw_ref, lng_ref, lnb_ref, o_ref, ns_ref):
    n_st = CONV_A_WIDTH - 1
    acc = g_ref[...] * w_ref[pl.ds(n_st, 1), :]
    for k in range(n_st):
        acc = acc + st_ref[k] * w_ref[pl.ds(k, 1), :]
    y = _ln_rows(acc + bdw_ref[...], lng_ref[...], lnb_ref[...])
    o_ref[...] = (y * jax.nn.sigmoid(y)).astype(BF16)
    _push_state(st_ref, g_ref[...], ns_ref)


def _conv_a_sample(state, layer, g, w_dw, b_dw, ln_g, ln_b, tb):
    n = g.shape[0]
    n_st = CONV_A_WIDTH - 1
    vspec = pl.BlockSpec((1, D_MODEL), lambda i: (0, 0))
    st_in, st_out = _state_specs(n_st, tb, layer)
    out, ns = pl.pallas_call(
        _conv_a_sample_kernel,
        grid=(n // tb,),
        in_specs=[st_in,
                  pl.BlockSpec((tb, D_MODEL), lambda i: (i, 0)),
                  pl.BlockSpec((32, D_MODEL), lambda i: (0, 0)),
                  vspec, vspec, vspec],
        out_specs=[pl.BlockSpec((tb, D_MODEL), lambda i: (i, 0)), st_out],
        out_shape=[jax.ShapeDtypeStruct((n, D_MODEL), BF16),
                   jax.ShapeDtypeStruct((1, n_st, n, D_MODEL), F32)],
        compiler_params=_params("arbitrary"),
        name="conv_a_sample",
    )(_rows_major(state), g, _pad_rows(w_dw, 32), _vec(b_dw), _vec(ln_g), _vec(ln_b))
    return out, _rows_major(ns)


POOL_HALO = 16


def _pool_prompt_kernel(x_ref, g_ref, w_ref, sc_ref, o_ref, hs_ref, ext_ref, diff_ref, wb_ref,
                        *, tt):
    t = pl.program_id(1)
    H = POOL_HALO
    _carry_halo(ext_ref, t, tt, H)

    def norm(r0):
        ext_ref[pl.ds(H + r0, NORM_ROWS), :] = _rms_rows(x_ref[0, pl.ds(r0, NORM_ROWS), :],
                                                         g_ref[...])
    _for_chunks(tt, NORM_ROWS, norm, unroll=NORM_UNROLL)

    hs_ref[0, 0] = ext_ref[pl.ds(tt, H), :]

    @pl.when((pl.program_id(0) == 0) & (t == 0))
    def _():
        wb_ref[...] = w_ref[...].astype(BF16)

    pr = 64
    win = pr + H
    for gi, w in enumerate(POOL_WINDOWS):
        lanes = pl.ds(gi * POOL_GROUP, POOL_GROUP)

        def rows(r0, w=w, lanes=lanes):
            s = ext_ref[pl.ds(r0, win), lanes]
            cur = s[H:]
            d = 1
            while d < w:
                s = s + pltpu.roll(s, d, 0)
                d *= 2
            pos = t * tt + r0 + lax.broadcasted_iota(jnp.int32, (pr, 1), 0)
            cnt = jnp.minimum(pos + 1, w).astype(F32)
            diff_ref[pl.ds(r0, pr), lanes] = (s[H:] / cnt - cur).astype(BF16)
        _for_chunks(tt, pr, rows)

    for gi in range(len(POOL_WINDOWS)):
        sl = slice(gi * POOL_GROUP, (gi + 1) * POOL_GROUP)
        y = jnp.dot(diff_ref[:, sl], wb_ref[gi], preferred_element_type=F32)
        o_ref[0, :, sl] = x_ref[0, :, sl] + y * sc_ref[:, sl]


def _pool_prompt(x3, g, w_grp, layer, scale):
    b, l, _ = x3.shape
    nt = l // TT
    ng = len(POOL_WINDOWS)
    vspec = pl.BlockSpec((1, D_MODEL), lambda i, t: (0, 0))
    return pl.pallas_call(
        functools.partial(_pool_prompt_kernel, tt=TT),
        grid=(b, nt),
        in_specs=[pl.BlockSpec((1, TT, D_MODEL), lambda i, t: (i, t, 0)),
                  vspec,
                  pl.BlockSpec((None, ng, POOL_GROUP, POOL_GROUP), lambda i, t: (layer, 0, 0, 0)),
                  vspec],
        out_specs=[pl.BlockSpec((1, TT, D_MODEL), lambda i, t: (i, t, 0)),
                   pl.BlockSpec((1, 1, POOL_HALO, D_MODEL), lambda i, t: (i, t, 0, 0))],
        out_shape=[jax.ShapeDtypeStruct((b, l, D_MODEL), F32),
                   jax.ShapeDtypeStruct((b, nt, POOL_HALO, D_MODEL), F32)],
        scratch_shapes=[pltpu.VMEM((TT + POOL_HALO, D_MODEL), F32),
                        pltpu.VMEM((TT, D_MODEL), BF16),
                        pltpu.VMEM((ng, POOL_GROUP, POOL_GROUP), BF16)],
        compiler_params=_params("arbitrary", "arbitrary"),
        name="pool_prompt",
    )(x3, _vec(g), w_grp, _vec(scale))


def _pool_sample_kernel(x_ref, st_ref, g_ref, w_ref, sc_ref, o_ref, ns_ref):
    x = x_ref[...]
    h = _rms_rows(x, g_ref[...])
    for gi, w in enumerate(POOL_WINDOWS):
        sl = slice(gi * POOL_GROUP, (gi + 1) * POOL_GROUP)
        cur = h[:, sl]
        s = cur
        for d in range(1, w):
            s = s + st_ref[POOL_STATE - d, :, sl]
        diff = s / float(min(PAST_LEN + 1, w)) - cur
        y = jnp.dot(diff.astype(BF16), w_ref[gi].astype(BF16), preferred_element_type=F32)
        o_ref[:, sl] = x[:, sl] + y * sc_ref[:, sl]
    _push_state(st_ref, h, ns_ref)


def _pool_sample(x, state, g, w_grp, layer, scale, tb):
    n = x.shape[0]
    ng = len(POOL_WINDOWS)
    vspec = pl.BlockSpec((1, D_MODEL), lambda i: (0, 0))
    row = pl.BlockSpec((tb, D_MODEL), lambda i: (i, 0))
    st_in, st_out = _state_specs(POOL_STATE, tb, layer)
    out, ns = pl.pallas_call(
        _pool_sample_kernel,
        grid=(n // tb,),
        in_specs=[row, st_in, vspec,
                  pl.BlockSpec((None, ng, POOL_GROUP, POOL_GROUP), lambda i: (layer, 0, 0, 0)),
                  vspec],
        out_specs=[row, st_out],
        out_shape=[jax.ShapeDtypeStruct((n, D_MODEL), F32),
                   jax.ShapeDtypeStruct((1, POOL_STATE, n, D_MODEL), F32)],
        compiler_params=_params("arbitrary"),
        name="pool_sample",
    )(x, _rows_major(state), _vec(g), w_grp, _vec(scale))
    return out, _rows_major(ns)


C_ROWS = 128


def _c_in_kernel(xp_ref, xs_ref, g_ref, wb_ref, wc_ref, wx_ref, wconv_ref,
                 z_ref, tail_ref, bgs_ref, cxs_ref, xnp_ref, xns_ref, ext_ref, carry_ref,
                 *, blocks_per_seq):
    i, j = pl.program_id(0), pl.program_id(1)
    last_i = pl.num_programs(0) - 1
    H = SUBLANES
    tm = xp_ref.shape[0]

    @pl.when(j == 0)
    def _():
        _rms_to_bf16(xp_ref, g_ref, xnp_ref)

    @pl.when((j == 0) & (i == last_i))
    def _():
        _rms_to_bf16(xs_ref, g_ref, xns_ref)

    def parts(xn_ref):
        xn = xn_ref[...]
        return tuple(jnp.dot(xn, w_ref[...], preferred_element_type=F32)
                     for w_ref in (wb_ref, wc_ref, wx_ref))

    bg, c, xv = parts(xnp_ref)
    cx = c * xv
    prev = carry_ref[j]
    ext_ref[pl.ds(0, H), :] = jnp.where(i % blocks_per_seq == 0, jnp.zeros_like(prev), prev)
    ext_ref[pl.ds(H, tm), :] = cx
    carry_ref[j] = cx[tm - H:]
    tail_ref[0] = cx[tm - H:]
    win = C_ROWS + H
    for r0 in range(0, tm, C_ROWS):
        w = ext_ref[pl.ds(r0, win), :]
        y = (pltpu.roll(w, 2, 0)[H:] * wconv_ref[pl.ds(0, 1), :]
             + pltpu.roll(w, 1, 0)[H:] * wconv_ref[pl.ds(1, 1), :]
             + w[H:] * wconv_ref[pl.ds(2, 1), :])
        z_ref[pl.ds(r0, C_ROWS), :] = (bg[r0:r0 + C_ROWS] * y).astype(BF16)

    @pl.when(i == last_i)
    def _():
        bs, cs, xvs = parts(xns_ref)
        bgs_ref[...] = bs
        cxs_ref[...] = cs * xvs


def _c_in_conv(xp, xs, g, w, w_conv, seq_len, tn):
    mp, ms = xp.shape[0], xs.shape[0]
    nj = D_MODEL // tn
    ni = mp // TM
    assert seq_len % TM == 0
    wspecs = [pl.BlockSpec((D_MODEL, tn), lambda i, j, p=p: (0, p * nj + j)) for p in range(3)]
    tile_s = pl.BlockSpec((ms, tn), lambda i, j: (0, _sample_col(i, j, ni)))
    return pl.pallas_call(
        functools.partial(_c_in_kernel, blocks_per_seq=seq_len // TM),
        grid=(ni, nj),
        in_specs=[pl.BlockSpec((TM, D_MODEL), lambda i, j: (i, 0)),
                  pl.BlockSpec((ms, D_MODEL), lambda i, j: (0, 0)),
                  pl.BlockSpec((1, D_MODEL), lambda i, j: (0, 0))] + wspecs
                 + [pl.BlockSpec((SUBLANES, tn), lambda i, j: (0, j))],
        out_specs=[pl.BlockSpec((TM, tn), lambda i, j: (i, j)),
                   pl.BlockSpec((1, SUBLANES, tn), lambda i, j: (i, 0, j)),
                   tile_s, tile_s],
        out_shape=[jax.ShapeDtypeStruct((mp, D_MODEL), BF16),
                   jax.ShapeDtypeStruct((ni, SUBLANES, D_MODEL), F32),
                   jax.ShapeDtypeStruct((ms, D_MODEL), F32),
                   jax.ShapeDtypeStruct((ms, D_MODEL), F32)],
        scratch_shapes=[pltpu.VMEM((TM, D_MODEL), BF16), pltpu.VMEM((ms, D_MODEL), BF16),
                        pltpu.VMEM((TM + SUBLANES, tn), F32),
                        pltpu.VMEM((nj, SUBLANES, tn), F32)],
        compiler_params=_params("arbitrary", "arbitrary"),
        name="c_in_conv",
    )(xp, xs, _vec(g), w, w, w, _pad_rows(w_conv, SUBLANES))


def _conv_c_sample_kernel(b_ref, cx_ref, st_ref, w_ref, o_ref, ns_ref):
    y = (st_ref[0] * w_ref[pl.ds(0, 1), :] + st_ref[1] * w_ref[pl.ds(1, 1), :]
         + cx_ref[...] * w_ref[pl.ds(2, 1), :])
    o_ref[...] = (b_ref[...] * y).astype(BF16)
    _push_state(st_ref, cx_ref[...], ns_ref)


def _conv_c_sample(bg, cx, state, layer, w_conv):
    n = bg.shape[0]
    n_st = CONV_C_WIDTH - 1
    row = pl.BlockSpec((n, D_MODEL), lambda i: (0, 0))
    st_in, st_out = _state_specs(n_st, n, layer)
    out, ns = pl.pallas_call(
        _conv_c_sample_kernel,
        grid=(1,),
        in_specs=[row, row, st_in, pl.BlockSpec((SUBLANES, D_MODEL), lambda i: (0, 0))],
        out_specs=[row, st_out],
        out_shape=[jax.ShapeDtypeStruct((n, D_MODEL), BF16),
                   jax.ShapeDtypeStruct((1, n_st, n, D_MODEL), F32)],
        compiler_params=_params("arbitrary"),
        name="conv_c_sample",
    )(bg, cx, _rows_major(state), _pad_rows(w_conv, SUBLANES))
    return out, _rows_major(ns)


def _sg_kernel(up_ref, vp_ref, us_ref, vs_ref, lng_ref, lnb_ref, ws_ref, bs_ref, ws0_ref, bs0_ref,
               wo_ref, xp_ref, xs_ref, op_ref, os_ref, vns_ref, vn_ref, gt_ref, wsm_ref):
    i = pl.program_id(0)
    tm = up_ref.shape[0]

    @pl.when(i == 0)
    def _():
        row = lax.broadcasted_iota(jnp.int32, (CHUNK, CHUNK), 0)
        col = lax.broadcasted_iota(jnp.int32, (CHUNK, CHUNK), 1)
        for hd in range(N_SG_HEADS):
            wsm_ref[hd] = jnp.where(col <= row, ws_ref[hd], 0.0).astype(BF16)

    def norm(r0):
        vn_ref[pl.ds(r0, NORM_ROWS), :] = _ln_rows(vp_ref[pl.ds(r0, NORM_ROWS), :], lng_ref[...],
                                                   lnb_ref[...]).astype(BF16)
    _for_chunks(tm, NORM_ROWS, norm, unroll=NORM_UNROLL)

    for r0 in range(0, tm, CHUNK):
        for hd in range(N_SG_HEADS):
            lanes = pl.ds(hd * SG_HEAD_DIM, SG_HEAD_DIM)
            mixed = jnp.dot(wsm_ref[hd], vn_ref[pl.ds(r0, CHUNK), lanes],
                            preferred_element_type=F32) + bs_ref[:, lanes]
            gt_ref[pl.ds(r0, CHUNK), lanes] = (up_ref[pl.ds(r0, CHUNK), lanes].astype(F32)
                                               * mixed).astype(BF16)

    op_ref[...] = xp_ref[...] + jnp.dot(gt_ref[...], wo_ref[...], preferred_element_type=F32)

    @pl.when(i == pl.num_programs(0) - 1)
    def _():
        vn = _ln_rows(vs_ref[...], lng_ref[...], lnb_ref[...])
        vns_ref[...] = vn
        mixed = ws0_ref[...] * vn + bs0_ref[...]
        gated = (us_ref[...].astype(F32) * mixed).astype(BF16)
        os_ref[...] = xs_ref[...] + jnp.dot(gated, wo_ref[...], preferred_element_type=F32)


def _sg(up, vp, us, vs, ln_g, ln_b, w_s, layer, bs_map, ws0_map, bs0_map, w_o, xp, xs):
    mp, ms = up.shape[0], us.shape[0]
    vspec = pl.BlockSpec((1, D_MODEL), lambda i: (0, 0))
    rowp = pl.BlockSpec((TM_FULL, D_MODEL), lambda i: (i, 0))
    rows = pl.BlockSpec((ms, D_MODEL), lambda i: (0, 0))
    return pl.pallas_call(
        _sg_kernel,
        grid=(mp // TM_FULL,),
        in_specs=[rowp, rowp, rows, rows, vspec, vspec,
                  pl.BlockSpec((None, N_SG_HEADS, CHUNK, CHUNK), lambda i: (layer, 0, 0, 0)),
                  pl.BlockSpec((CHUNK, D_MODEL), lambda i: (0, 0)),
                  vspec, vspec,
                  pl.BlockSpec((D_MODEL, D_MODEL), lambda i: (0, 0), pipeline_mode=pl.Buffered(1)),
                  rowp, rows],
        out_specs=[rowp, rows, rows],
        out_shape=[jax.ShapeDtypeStruct((mp, D_MODEL), F32),
                   jax.ShapeDtypeStruct((ms, D_MODEL), F32),
                   jax.ShapeDtypeStruct((ms, D_MODEL), F32)],
        scratch_shapes=[pltpu.VMEM((TM_FULL, D_MODEL), BF16), pltpu.VMEM((TM_FULL, D_MODEL), BF16),
                        pltpu.VMEM((N_SG_HEADS, CHUNK, CHUNK), BF16)],
        compiler_params=_params("arbitrary"),
        name="sg_mix",
    )(up, vp, us, vs, _vec(ln_g), _vec(ln_b), w_s, bs_map, ws0_map, bs0_map, w_o, xp, xs)


def kernel(x_prompt, x_sample, state_conv_a, state_pool, state_short_conv, norm_mix, norm_mlp, norm_final, a_w_pw1, a_b_pw1, a_w_dw, a_b_dw, a_ln_g, a_ln_b, a_w_pw2, b_w_grp, b_scale, c_w_in, c_w_conv, c_w_out, d_w_uv, d_ln_g, d_ln_b, d_w_s, d_b_s, d_w_o, mlp_w1, mlp_w2):
    b, l, _ = x_prompt.shape
    mp = b * l
    ms = x_sample.shape[0]
    assert x_sample.shape[1] == 1 and mp % TM == 0 and l % TT == 0
    flat3 = lambda a: a.reshape(b, l, D_MODEL)
    xp = x_prompt.reshape(mp, D_MODEL)
    xs = x_sample.reshape(ms, D_MODEL)
    tf = 512

    (gp,), (gs,), (w_pw2,) = _mm_parts(
        "a_pw1_glu", xp, xs, norm_mix[0], a_w_pw1.astype(BF16), 0, a_b_pw1[0], 2, (F32,),
        _glu_epilogue, 512, cast_jobs=[(a_w_pw2, 0)])
    cp3, (w1_0, w2_0) = _conv_a_prompt(flat3(gp), a_w_dw[0], a_b_dw[0], a_ln_g[0], a_ln_b[0],
                                       cast_jobs=[(mlp_w1, 0), (mlp_w2, 0)])
    cp = cp3.reshape(mp, D_MODEL)
    cs, sa = _conv_a_sample(state_conv_a, 0, gs, a_w_dw[0], a_b_dw[0], a_ln_g[0], a_ln_b[0], 32)
    pa = flat3(gp)[:, l - (CONV_A_WIDTH - 1):]
    xp, xs = _mm_res("a_pw2", cp, cs, w_pw2, xp, xs)
    xp, xs, (w1_1, w2_1, w_in, w_out) = _mlp(
        "mlp0", xp, xs, norm_mlp[0], w1_0, w2_0, None, tf,
        cast_jobs=[(mlp_w1, 1), (mlp_w2, 1), (c_w_in, 0), (c_w_out, 0)])

    xp3, hs = _pool_prompt(flat3(xp), norm_mix[1], b_w_grp, 0, b_scale[0])
    xp = xp3.reshape(mp, D_MODEL)
    pb = hs[:, -1, POOL_HALO - POOL_STATE:]
    xs, sb = _pool_sample(xs, state_pool, norm_mix[1], b_w_grp, 0, b_scale[0], 32)
    xp, xs, (w1_2, w2_2, w_uv, w_o) = _mlp(
        "mlp1", xp, xs, norm_mlp[1], w1_1, w2_1, None, tf,
        cast_jobs=[(mlp_w1, 2), (mlp_w2, 2), (d_w_uv, 0), (d_w_o, 0)])

    zp, cx_tail, bgs, cxs = _c_in_conv(xp, xs, norm_mix[2], w_in, c_w_conv[0], l, 512)
    zs, sc = _conv_c_sample(bgs, cxs, state_short_conv, 0, c_w_conv[0])
    pc = cx_tail.reshape(b, l // TM, SUBLANES, D_MODEL)[:, -1, SUBLANES - (CONV_C_WIDTH - 1):]
    xp, xs = _mm_res("c_out", zp, zs, w_out, xp, xs)
    xp, xs, (w1_3, w2_3) = _mlp("mlp2", xp, xs, norm_mlp[2], w1_2, w2_2, None, tf,
                                cast_jobs=[(mlp_w1, 3), (mlp_w2, 3)])

    (up, vp), (us, vs), _ = _mm_parts("d_uv_gelu", xp, xs, norm_mix[3], w_uv, None, None, 2,
                                      (BF16, F32), _gelu_epilogue, 1024)
    bs_map = jnp.repeat(d_b_s[0].T, SG_HEAD_DIM, axis=1)
    ws0_map = _vec(jnp.repeat(d_w_s[0, :, 0, 0], SG_HEAD_DIM))
    bs0_map = _vec(jnp.repeat(d_b_s[0, :, 0], SG_HEAD_DIM))
    xp, xs, sd = _sg(up, vp, us, vs, d_ln_g[0], d_ln_b[0], d_w_s, 0, bs_map, ws0_map, bs0_map,
                     w_o, xp, xs)
    yp, ys, _ = _mlp("mlp3", xp, xs, norm_mlp[3], w1_3, w2_3, None, tf, final_g=norm_final)

    return (yp.reshape(b, l, D_MODEL), ys.reshape(ms, 1, D_MODEL),
            pa[None], sa, pb[None], sb, pc[None], sc,
            sd.reshape(1, ms, 1, D_MODEL))
```

```python
import functools

import jax
import jax.numpy as jnp
from jax import lax
from jax.experimental import pallas as pl
from jax.experimental.pallas import tpu as pltpu

D_MODEL = 2048
D_FF = 4 * D_MODEL
CONV_A_WIDTH = 31
POOL_WINDOWS = (2, 4, 8, 16)
POOL_GROUP = D_MODEL // len(POOL_WINDOWS)
POOL_STATE = max(POOL_WINDOWS) - 1
CONV_C_WIDTH = 3
CHUNK = 128
N_SG_HEADS = 8
SG_HEAD_DIM = D_MODEL // N_SG_HEADS
PAST_LEN = 16384
RMS_EPS = 1e-6
LN_EPS = 1e-5

SUBLANES = 8
BF16_ROWS = 16
VMEM_LIMIT_BYTES = 60000 * 1024
HALO = 32

TM = 1024
TM_FULL = 512
TT = 256
TT_POOL = 512
NORM_ROWS = 32
NORM_UNROLL = 4

F32 = jnp.float32
BF16 = jnp.bfloat16


def _params(*sem):
    return pltpu.CompilerParams(dimension_semantics=sem, vmem_limit_bytes=VMEM_LIMIT_BYTES)


def _for_chunks(n, chunk, body, unroll=1):
    assert n % chunk == 0
    if n == chunk:
        body(0)
    else:
        def step(i, c):
            body(pl.multiple_of(i * chunk, chunk))
            return c
        lax.fori_loop(0, n // chunk, step, 0, unroll=unroll)


def _rms_rows(x, g):
    r = lax.rsqrt(jnp.mean(x * x, axis=-1, keepdims=True) + RMS_EPS)
    return x * r * g


def _ln_rows(x, g, b):
    mu = jnp.mean(x, axis=-1, keepdims=True)
    xc = x - mu
    var = jnp.mean(xc * xc, axis=-1, keepdims=True)
    return xc * lax.rsqrt(var + LN_EPS) * g + b


def _row_chunk(tm):
    for c in (64, 32, 16, 8):
        if tm % c == 0:
            return c
    raise ValueError(tm)


def _rms_to_bf16(x_ref, g_ref, xn_ref):
    n = x_ref.shape[0]
    rc = _row_chunk(n)

    def body(r0):
        xn_ref[pl.ds(r0, rc), :] = _rms_rows(x_ref[pl.ds(r0, rc), :], g_ref[...]).astype(BF16)
    _for_chunks(n, rc, body, unroll=2)


def _vec(v):
    return v.reshape(1, D_MODEL)


def _sample_col(i, j, ni):
    return jnp.where(i == ni - 1, j, 0)


def _wspec(block, index, layer, **kw):
    if layer is None:
        return pl.BlockSpec(block, index, **kw)
    return pl.BlockSpec((None,) + block, lambda *g: (layer,) + tuple(index(*g)), **kw)


def _cast_job_specs(jobs, n_steps, step_of):
    in_specs, args, out_specs, out_shapes = [], [], [], []
    for src, layer in jobs:
        _, r, c = src.shape
        assert r % (n_steps * BF16_ROWS) == 0, (src.shape, n_steps)
        pr = r // n_steps
        in_specs.append(pl.BlockSpec((None, pr, c),
                                     lambda *g, layer=layer: (layer, step_of(*g), 0)))
        args.append(src)
        out_specs.append(pl.BlockSpec((pr, c), lambda *g: (step_of(*g), 0)))
        out_shapes.append(jax.ShapeDtypeStruct((r, c), BF16))
    return in_specs, args, out_specs, out_shapes


def _run_cast_jobs(src_refs, dst_refs):
    for s_ref, d_ref in zip(src_refs, dst_refs):
        d_ref[...] = s_ref[...].astype(BF16)


def _mm_parts_kernel(*refs, n_parts, n_out, has_bias, n_jobs, epilogue):
    xp_ref, xs_ref, g_ref = refs[:3]
    w_refs = refs[3:3 + n_parts]
    pos = 3 + n_parts
    b_refs = refs[pos:pos + n_parts] if has_bias else ()
    pos += n_parts if has_bias else 0
    job_src = refs[pos:pos + n_jobs]
    pos += n_jobs
    op_refs = refs[pos:pos + n_out]
    os_refs = refs[pos + n_out:pos + 2 * n_out]
    pos += 2 * n_out
    job_dst = refs[pos:pos + n_jobs]
    xnp_ref, xns_ref = refs[pos + n_jobs:]
    i, j = pl.program_id(0), pl.program_id(1)
    last_i = pl.num_programs(0) - 1

    @pl.when(j == 0)
    def _():
        _rms_to_bf16(xp_ref, g_ref, xnp_ref)

    @pl.when((j == 0) & (i == last_i))
    def _():
        _rms_to_bf16(xs_ref, g_ref, xns_ref)

    def rows(xn_ref, o_refs):
        parts = []
        for p in range(n_parts):
            a = jnp.dot(xn_ref[...], w_refs[p][...], preferred_element_type=F32)
            if has_bias:
                a = a + b_refs[p][...]
            parts.append(a)
        for o_ref, o in zip(o_refs, epilogue(*parts)):
            o_ref[...] = o.astype(o_ref.dtype)

    rows(xnp_ref, op_refs)
    _run_cast_jobs(job_src, job_dst)

    @pl.when(i == last_i)
    def _():
        rows(xns_ref, os_refs)


def _mm_parts(name, xp, xs, g, w, layer, bias, n_parts, out_dtypes, epilogue, tn, cast_jobs=()):
    mp, ms = xp.shape[0], xs.shape[0]
    nj = D_MODEL // tn
    ni = mp // TM
    in_specs = [pl.BlockSpec((TM, D_MODEL), lambda i, j: (i, 0)),
                pl.BlockSpec((ms, D_MODEL), lambda i, j: (0, 0)),
                pl.BlockSpec((1, D_MODEL), lambda i, j: (0, 0))]
    args = [xp, xs, _vec(g)]
    for p in range(n_parts):
        in_specs.append(_wspec((D_MODEL, tn), lambda i, j, p=p: (0, p * nj + j), layer))
        args.append(w)
    if bias is not None:
        for p in range(n_parts):
            in_specs.append(pl.BlockSpec((1, tn), lambda i, j, p=p: (0, p * nj + j)))
            args.append(bias.reshape(1, n_parts * D_MODEL))
    j_in, j_args, j_out, j_shapes = _cast_job_specs(cast_jobs, ni * nj, lambda i, j: i * nj + j)
    kern = functools.partial(_mm_parts_kernel, n_parts=n_parts, n_out=len(out_dtypes),
                             has_bias=bias is not None, n_jobs=len(cast_jobs), epilogue=epilogue)
    outs = pl.pallas_call(
        kern,
        grid=(ni, nj),
        in_specs=in_specs + j_in,
        out_specs=([pl.BlockSpec((TM, tn), lambda i, j: (i, j)) for _ in out_dtypes]
                   + [pl.BlockSpec((ms, tn), lambda i, j: (0, _sample_col(i, j, ni)))
                      for _ in out_dtypes] + j_out),
        out_shape=([jax.ShapeDtypeStruct((mp, D_MODEL), dt) for dt in out_dtypes]
                   + [jax.ShapeDtypeStruct((ms, D_MODEL), dt) for dt in out_dtypes] + j_shapes),
        scratch_shapes=[pltpu.VMEM((TM, D_MODEL), BF16), pltpu.VMEM((ms, D_MODEL), BF16)],
        compiler_params=_params("arbitrary", "arbitrary"),
        name=name,
    )(*args, *j_args)
    n = len(out_dtypes)
    return outs[:n], outs[n:2 * n], outs[2 * n:]


def _glu_epilogue(a, b):
    return (a * jax.nn.sigmoid(b),)


def _gelu_epilogue(a, b):
    return (jax.nn.gelu(a), jax.nn.gelu(b))


def _mm_res_kernel(ap_ref, as_ref, w_ref, xp_ref, xs_ref, op_ref, os_ref):
    op_ref[...] = xp_ref[...] + jnp.dot(ap_ref[...], w_ref[...], preferred_element_type=F32)

    @pl.when(pl.program_id(0) == pl.num_programs(0) - 1)
    def _():
        os_ref[...] = xs_ref[...] + jnp.dot(as_ref[...], w_ref[...], preferred_element_type=F32)


def _mm_res(name, ap, as_, w, xp, xs):
    mp, ms = ap.shape[0], as_.shape[0]
    rowp = pl.BlockSpec((TM_FULL, D_MODEL), lambda i: (i, 0))
    rows = pl.BlockSpec((ms, D_MODEL), lambda i: (0, 0))
    return pl.pallas_call(
        _mm_res_kernel,
        grid=(mp // TM_FULL,),
        in_specs=[rowp, rows,
                  pl.BlockSpec((D_MODEL, D_MODEL), lambda i: (0, 0), pipeline_mode=pl.Buffered(1)),
                  rowp, rows],
        out_specs=[rowp, rows],
        out_shape=[jax.ShapeDtypeStruct((mp, D_MODEL), F32),
                   jax.ShapeDtypeStruct((ms, D_MODEL), F32)],
        compiler_params=_params("arbitrary"),
        name=name,
    )(ap, as_, w, xp, xs)


def _mlp_kernel(*refs, final, n_jobs):
    n_in = 6 if final else 5
    xp_ref, xs_ref, g_ref, w1_ref, w2_ref = refs[:5]
    gf_ref = refs[5] if final else None
    job_src = refs[n_in:n_in + n_jobs]
    op_ref, os_ref = refs[n_in + n_jobs:n_in + n_jobs + 2]
    job_dst = refs[n_in + n_jobs + 2:n_in + 2 * n_jobs + 2]
    xnp_ref, xns_ref = refs[n_in + 2 * n_jobs + 2:]
    i, j = pl.program_id(0), pl.program_id(1)
    last_i = pl.num_programs(0) - 1
    last_j = pl.num_programs(1) - 1

    def rows(xn_ref, x_ref, o_ref, first):
        h = jnp.dot(xn_ref[...], w1_ref[...], preferred_element_type=F32)
        h = jnp.square(jnp.maximum(h, 0.0)).astype(BF16)
        y = jnp.dot(h, w2_ref[...], preferred_element_type=F32)
        if first:
            o_ref[...] = x_ref[...] + y
        else:
            o_ref[...] += y

    @pl.when(j == 0)
    def _():
        _rms_to_bf16(xp_ref, g_ref, xnp_ref)
        rows(xnp_ref, xp_ref, op_ref, True)
        _run_cast_jobs(job_src, job_dst)

    @pl.when(j > 0)
    def _():
        rows(xnp_ref, xp_ref, op_ref, False)
        _run_cast_jobs(job_src, job_dst)

    @pl.when((i == last_i) & (j == 0))
    def _():
        _rms_to_bf16(xs_ref, g_ref, xns_ref)
        rows(xns_ref, xs_ref, os_ref, True)

    @pl.when((i == last_i) & (j > 0))
    def _():
        rows(xns_ref, xs_ref, os_ref, False)

    if final:
        def norm(o_ref):
            n = o_ref.shape[0]
            rc = _row_chunk(n)

            def body(r0):
                o_ref[pl.ds(r0, rc), :] = _rms_rows(o_ref[pl.ds(r0, rc), :], gf_ref[...])
            _for_chunks(n, rc, body, unroll=2)

        @pl.when(j == last_j)
        def _():
            norm(op_ref)

        @pl.when((j == last_j) & (i == last_i))
        def _():
            norm(os_ref)


def _mlp(name, xp, xs, g, w1, w2, layer, tf, final_g=None, cast_jobs=()):
    mp, ms = xp.shape[0], xs.shape[0]
    final = final_g is not None
    ni, nj = mp // TM, D_FF // tf
    in_specs = [pl.BlockSpec((TM, D_MODEL), lambda i, j: (i, 0)),
                pl.BlockSpec((ms, D_MODEL), lambda i, j: (0, 0)),
                pl.BlockSpec((1, D_MODEL), lambda i, j: (0, 0)),
                _wspec((D_MODEL, tf), lambda i, j: (0, j), layer),
                _wspec((tf, D_MODEL), lambda i, j: (j, 0), layer)]
    args = [xp, xs, _vec(g), w1, w2]
    if final:
        in_specs.append(pl.BlockSpec((1, D_MODEL), lambda i, j: (0, 0)))
        args.append(_vec(final_g))
    j_in, j_args, j_out, j_shapes = _cast_job_specs(cast_jobs, ni * nj, lambda i, j: i * nj + j)
    outs = pl.pallas_call(
        functools.partial(_mlp_kernel, final=final, n_jobs=len(cast_jobs)),
        grid=(ni, nj),
        in_specs=in_specs + j_in,
        out_specs=[pl.BlockSpec((TM, D_MODEL), lambda i, j: (i, 0)),
                   pl.BlockSpec((ms, D_MODEL), lambda i, j: (0, 0))] + j_out,
        out_shape=[jax.ShapeDtypeStruct((mp, D_MODEL), F32),
                   jax.ShapeDtypeStruct((ms, D_MODEL), F32)] + j_shapes,
        scratch_shapes=[pltpu.VMEM((TM, D_MODEL), BF16), pltpu.VMEM((ms, D_MODEL), BF16)],
        compiler_params=_params("arbitrary", "arbitrary"),
        name=name,
    )(*args, *j_args)
    return outs[0], outs[1], outs[2:]


CONV_ROWS = 128
CONV_LANES = 256


def _carry_halo(ext_ref, t, tt, halo):
    @pl.when(t == 0)
    def _():
        ext_ref[pl.ds(0, halo), :] = jnp.zeros((halo, D_MODEL), F32)

    @pl.when(t > 0)
    def _():
        ext_ref[pl.ds(0, halo), :] = ext_ref[pl.ds(tt, halo), :]


def _conv_a_kernel(*refs, tt, n_jobs):
    g_ref, w_ref, bdw_ref, lng_ref, lnb_ref = refs[:5]
    job_src = refs[5:5 + n_jobs]
    o_ref = refs[5 + n_jobs]
    job_dst = refs[6 + n_jobs:6 + 2 * n_jobs]
    ext_ref, c_ref = refs[6 + 2 * n_jobs:]
    _run_cast_jobs(job_src, job_dst)
    _carry_halo(ext_ref, pl.program_id(1), tt, HALO)

    def copy_in(r0):
        ext_ref[pl.ds(HALO + r0, 64), :] = g_ref[0, pl.ds(r0, 64), :]
    _for_chunks(tt, 64, copy_in)

    off = HALO - (CONV_A_WIDTH - 1)
    win = CONV_ROWS + HALO
    tiles = CONV_ROWS // SUBLANES
    for c0 in range(0, D_MODEL, CONV_LANES):
        lanes = pl.ds(c0, CONV_LANES)

        def rows(r0, lanes=lanes):
            w = ext_ref[pl.ds(r0, win), lanes]
            acc = jnp.zeros((tiles, SUBLANES, CONV_LANES), F32)
            for s in range(SUBLANES):
                ws = w if s == 0 else pltpu.roll(w, win - s, 0)
                for k in range(CONV_A_WIDTH):
                    if (off + k) % SUBLANES == s:
                        a = off + k - s
                        tap = ws[a:a + CONV_ROWS].reshape(tiles, SUBLANES, CONV_LANES)
                        acc = acc + tap * w_ref[k, :, lanes][None]
            c_ref[pl.ds(r0, CONV_ROWS), lanes] = acc.reshape(CONV_ROWS, CONV_LANES)
        _for_chunks(tt, CONV_ROWS, rows)

    def norm(r0):
        c = c_ref[pl.ds(r0, NORM_ROWS), :] + bdw_ref[...]
        y = _ln_rows(c, lng_ref[...], lnb_ref[...])
        o_ref[0, pl.ds(r0, NORM_ROWS), :] = (y * jax.nn.sigmoid(y)).astype(BF16)
    _for_chunks(tt, NORM_ROWS, norm, unroll=NORM_UNROLL)


def _pad_rows(w, rows):
    return jnp.zeros((rows, D_MODEL), F32).at[:w.shape[0]].set(w)


def _conv_a_prompt(g3, w_dw, b_dw, ln_g, ln_b, cast_jobs=()):
    b, l, _ = g3.shape
    nt = l // TT
    vspec = pl.BlockSpec((1, D_MODEL), lambda i, t: (0, 0))
    w_rep = jnp.broadcast_to(_pad_rows(w_dw, 32)[:, None, :], (32, SUBLANES, D_MODEL))
    j_in, j_args, j_out, j_shapes = _cast_job_specs(cast_jobs, b * nt, lambda i, t: i * nt + t)
    outs = pl.pallas_call(
        functools.partial(_conv_a_kernel, tt=TT, n_jobs=len(cast_jobs)),
        grid=(b, nt),
        in_specs=[pl.BlockSpec((1, TT, D_MODEL), lambda i, t: (i, t, 0)),
                  pl.BlockSpec((32, SUBLANES, D_MODEL), lambda i, t: (0, 0, 0)),
                  vspec, vspec, vspec] + j_in,
        out_specs=[pl.BlockSpec((1, TT, D_MODEL), lambda i, t: (i, t, 0))] + j_out,
        out_shape=[jax.ShapeDtypeStruct((b, l, D_MODEL), BF16)] + j_shapes,
        scratch_shapes=[pltpu.VMEM((TT + HALO, D_MODEL), F32),
                        pltpu.VMEM((TT, D_MODEL), F32)],
        compiler_params=_params("arbitrary", "arbitrary"),
        name="conv_a_prompt",
    )(g3, w_rep, _vec(b_dw), _vec(ln_g), _vec(ln_b), *j_args)
    return outs[0], outs[1:]


def _rows_major(state):
    return jnp.transpose(state, (0, 2, 1, 3))


def _push_state(st_ref, new_row, ns_ref):
    n_st = st_ref.shape[0]
    for k in range(n_st - 1):
        ns_ref[k] = st_ref[k + 1]
    ns_ref[n_st - 1] = new_row


def _state_specs(n_st, tb, layer):
    return (pl.BlockSpec((None, n_st, tb, D_MODEL), lambda i: (layer, 0, i, 0)),
            pl.BlockSpec((None, n_st, tb, D_MODEL), lambda i: (0, 0, i, 0)))


def _conv_a_sample_kernel(st_ref, g_ref, w_ref, bdw_ref, lng_ref, lnb_ref, o_ref, ns_ref):
    n_st = CONV_A_WIDTH - 1
    acc = g_ref[...] * w_ref[pl.ds(n_st, 1), :]
    for k in range(n_st):
        acc = acc + st_ref[k] * w_ref[pl.ds(k, 1), :]
    y = _ln_rows(acc + bdw_ref[...], lng_ref[...], lnb_ref[...])
    o_ref[...] = (y * jax.nn.sigmoid(y)).astype(BF16)
    _push_state(st_ref, g_ref[...], ns_ref)


def _conv_a_sample(state, layer, g, w_dw, b_dw, ln_g, ln_b, tb):
    n = g.shape[0]
    n_st = CONV_A_WIDTH - 1
    vspec = pl.BlockSpec((1, D_MODEL), lambda i: (0, 0))
    st_in, st_out = _state_specs(n_st, tb, layer)
    out, ns = pl.pallas_call(
        _conv_a_sample_kernel,
        grid=(n // tb,),
        in_specs=[st_in,
                  pl.BlockSpec((tb, D_MODEL), lambda i: (i, 0)),
                  pl.BlockSpec((32, D_MODEL), lambda i: (0, 0)),
                  vspec, vspec, vspec],
        out_specs=[pl.BlockSpec((tb, D_MODEL), lambda i: (i, 0)), st_out],
        out_shape=[jax.ShapeDtypeStruct((n, D_MODEL), BF16),
                   jax.ShapeDtypeStruct((1, n_st, n, D_MODEL), F32)],
        compiler_params=_params("arbitrary"),
        name="conv_a_sample",
    )(_rows_major(state), g, _pad_rows(w_dw, 32), _vec(b_dw), _vec(ln_g), _vec(ln_b))
    return out, _rows_major(ns)


POOL_HALO = 16


def _pool_prompt_kernel(x_ref, g_ref, w_ref, sc_ref, o_ref, hs_ref, ext_ref, diff_ref, wb_ref,
                        *, tt):
    t = pl.program_id(1)
    H = POOL_HALO
    _carry_halo(ext_ref, t, tt, H)

    def norm(r0):
        ext_ref[pl.ds(H + r0, NORM_ROWS), :] = _rms_rows(x_ref[0, pl.ds(r0, NORM_ROWS), :],
                                                         g_ref[...])
    _for_chunks(tt, NORM_ROWS, norm, unroll=NORM_UNROLL)

    hs_ref[0, 0] = ext_ref[pl.ds(tt, H), :]

    @pl.when((pl.program_id(0) == 0) & (t == 0))
    def _():
        wb_ref[...] = w_ref[...].astype(BF16)

    pr = 64
    win = pr + H
    for gi, w in enumerate(POOL_WINDOWS):
        lanes = pl.ds(gi * POOL_GROUP, POOL_GROUP)

        def rows(r0, w=w, lanes=lanes):
            s = ext_ref[pl.ds(r0, win), lanes]
            cur = s[H:]
            d = 1
            while d < w:
                s = s + pltpu.roll(s, d, 0)
                d *= 2
            pos = t * tt + r0 + lax.broadcasted_iota(jnp.int32, (pr, 1), 0)
            cnt = jnp.minimum(pos + 1, w).astype(F32)
            diff_ref[pl.ds(r0, pr), lanes] = (s[H:] / cnt - cur).astype(BF16)
        _for_chunks(tt, pr, rows)

    for gi in range(len(POOL_WINDOWS)):
        sl = slice(gi * POOL_GROUP, (gi + 1) * POOL_GROUP)
        y = jnp.dot(diff_ref[:, sl], wb_ref[gi], preferred_element_type=F32)
        o_ref[0, :, sl] = x_ref[0, :, sl] + y * sc_ref[:, sl]


def _pool_prompt(x3, g, w_grp, layer, scale):
    b, l, _ = x3.shape
    tt = TT_POOL
    nt = l // tt
    ng = len(POOL_WINDOWS)
    vspec = pl.BlockSpec((1, D_MODEL), lambda i, t: (0, 0))
    return pl.pallas_call(
        functools.partial(_pool_prompt_kernel, tt=tt),
        grid=(b, nt),
        in_specs=[pl.BlockSpec((1, tt, D_MODEL), lambda i, t: (i, t, 0)),
                  vspec,
                  pl.BlockSpec((None, ng, POOL_GROUP, POOL_GROUP), lambda i, t: (layer, 0, 0, 0)),
                  vspec],
        out_specs=[pl.BlockSpec((1, tt, D_MODEL), lambda i, t: (i, t, 0)),
                   pl.BlockSpec((1, 1, POOL_HALO, D_MODEL), lambda i, t: (i, t, 0, 0))],
        out_shape=[jax.ShapeDtypeStruct((b, l, D_MODEL), F32),
                   jax.ShapeDtypeStruct((b, nt, POOL_HALO, D_MODEL), F32)],
        scratch_shapes=[pltpu.VMEM((tt + POOL_HALO, D_MODEL), F32),
                        pltpu.VMEM((tt, D_MODEL), BF16),
                        pltpu.VMEM((ng, POOL_GROUP, POOL_GROUP), BF16)],
        compiler_params=_params("arbitrary", "arbitrary"),
        name="pool_prompt",
    )(x3, _vec(g), w_grp, _vec(scale))


def _pool_sample_kernel(x_ref, st_ref, g_ref, w_ref, sc_ref, o_ref, ns_ref):
    x = x_ref[...]
    h = _rms_rows(x, g_ref[...])
    for gi, w in enumerate(POOL_WINDOWS):
        sl = slice(gi * POOL_GROUP, (gi + 1) * POOL_GROUP)
        cur = h[:, sl]
        s = cur
        for d in range(1, w):
            s = s + st_ref[POOL_STATE - d, :, sl]
        diff = s / float(min(PAST_LEN + 1, w)) - cur
        y = jnp.dot(diff.astype(BF16), w_ref[gi].astype(BF16), preferred_element_type=F32)
        o_ref[:, sl] = x[:, sl] + y * sc_ref[:, sl]
    _push_state(st_ref, h, ns_ref)


def _pool_sample(x, state, g, w_grp, layer, scale, tb):
    n = x.shape[0]
    ng = len(POOL_WINDOWS)
    vspec = pl.BlockSpec((1, D_MODEL), lambda i: (0, 0))
    row = pl.BlockSpec((tb, D_MODEL), lambda i: (i, 0))
    st_in, st_out = _state_specs(POOL_STATE, tb, layer)
    out, ns = pl.pallas_call(
        _pool_sample_kernel,
        grid=(n // tb,),
        in_specs=[row, st_in, vspec,
                  pl.BlockSpec((None, ng, POOL_GROUP, POOL_GROUP), lambda i: (layer, 0, 0, 0)),
                  vspec],
        out_specs=[row, st_out],
        out_shape=[jax.ShapeDtypeStruct((n, D_MODEL), F32),
                   jax.ShapeDtypeStruct((1, POOL_STATE, n, D_MODEL), F32)],
        compiler_params=_params("arbitrary"),
        name="pool_sample",
    )(x, _rows_major(state), _vec(g), w_grp, _vec(scale))
    return out, _rows_major(ns)


C_ROWS = 128


def _c_in_kernel(xp_ref, xs_ref, g_ref, wb_ref, wc_ref, wx_ref, wconv_ref,
                 z_ref, tail_ref, bgs_ref, cxs_ref, xnp_ref, xns_ref, ext_ref, carry_ref,
                 *, blocks_per_seq):
    i, j = pl.program_id(0), pl.program_id(1)
    last_i = pl.num_programs(0) - 1
    H = SUBLANES
    tm = xp_ref.shape[0]

    @pl.when(j == 0)
    def _():
        _rms_to_bf16(xp_ref, g_ref, xnp_ref)

    @pl.when((j == 0) & (i == last_i))
    def _():
        _rms_to_bf16(xs_ref, g_ref, xns_ref)

    def parts(xn_ref):
        xn = xn_ref[...]
        return tuple(jnp.dot(xn, w_ref[...], preferred_element_type=F32)
                     for w_ref in (wb_ref, wc_ref, wx_ref))

    bg, c, xv = parts(xnp_ref)
    cx = c * xv
    prev = carry_ref[j]
    ext_ref[pl.ds(0, H), :] = jnp.where(i % blocks_per_seq == 0, jnp.zeros_like(prev), prev)
    ext_ref[pl.ds(H, tm), :] = cx
    carry_ref[j] = cx[tm - H:]
    tail_ref[0] = cx[tm - H:]
    win = C_ROWS + H
    for r0 in range(0, tm, C_ROWS):
        w = ext_ref[pl.ds(r0, win), :]
        y = (pltpu.roll(w, 2, 0)[H:] * wconv_ref[pl.ds(0, 1), :]
             + pltpu.roll(w, 1, 0)[H:] * wconv_ref[pl.ds(1, 1), :]
             + w[H:] * wconv_ref[pl.ds(2, 1), :])
        z_ref[pl.ds(r0, C_ROWS), :] = (bg[r0:r0 + C_ROWS] * y).astype(BF16)

    @pl.when(i == last_i)
    def _():
        bs, cs, xvs = parts(xns_ref)
        bgs_ref[...] = bs
        cxs_ref[...] = cs * xvs


def _c_in_conv(xp, xs, g, w, w_conv, seq_len, tn):
    mp, ms = xp.shape[0], xs.shape[0]
    nj = D_MODEL // tn
    ni = mp // TM
    assert seq_len % TM == 0
    wspecs = [pl.BlockSpec((D_MODEL, tn), lambda i, j, p=p: (0, p * nj + j)) for p in range(3)]
    tile_s = pl.BlockSpec((ms, tn), lambda i, j: (0, _sample_col(i, j, ni)))
    return pl.pallas_call(
        functools.partial(_c_in_kernel, blocks_per_seq=seq_len // TM),
        grid=(ni, nj),
        in_specs=[pl.BlockSpec((TM, D_MODEL), lambda i, j: (i, 0)),
                  pl.BlockSpec((ms, D_MODEL), lambda i, j: (0, 0)),
                  pl.BlockSpec((1, D_MODEL), lambda i, j: (0, 0))] + wspecs
                 + [pl.BlockSpec((SUBLANES, tn), lambda i, j: (0, j))],
        out_specs=[pl.BlockSpec((TM, tn), lambda i, j: (i, j)),
                   pl.BlockSpec((1, SUBLANES, tn), lambda i, j: (i, 0, j)),
                   tile_s, tile_s],
        out_shape=[jax.ShapeDtypeStruct((mp, D_MODEL), BF16),
                   jax.ShapeDtypeStruct((ni, SUBLANES, D_MODEL), F32),
                   jax.ShapeDtypeStruct((ms, D_MODEL), F32),
                   jax.ShapeDtypeStruct((ms, D_MODEL), F32)],
        scratch_shapes=[pltpu.VMEM((TM, D_MODEL), BF16), pltpu.VMEM((ms, D_MODEL), BF16),
                        pltpu.VMEM((TM + SUBLANES, tn), F32),
                        pltpu.VMEM((nj, SUBLANES, tn), F32)],
        compiler_params=_params("arbitrary", "arbitrary"),
        name="c_in_conv",
    )(xp, xs, _vec(g), w, w, w, _pad_rows(w_conv, SUBLANES))


def _conv_c_sample_kernel(b_ref, cx_ref, st_ref, w_ref, o_ref, ns_ref):
    y = (st_ref[0] * w_ref[pl.ds(0, 1), :] + st_ref[1] * w_ref[pl.ds(1, 1), :]
         + cx_ref[...] * w_ref[pl.ds(2, 1), :])
    o_ref[...] = (b_ref[...] * y).astype(BF16)
    _push_state(st_ref, cx_ref[...], ns_ref)


def _conv_c_sample(bg, cx, state, layer, w_conv):
    n = bg.shape[0]
    n_st = CONV_C_WIDTH - 1
    row = pl.BlockSpec((n, D_MODEL), lambda i: (0, 0))
    st_in, st_out = _state_specs(n_st, n, layer)
    out, ns = pl.pallas_call(
        _conv_c_sample_kernel,
        grid=(1,),
        in_specs=[row, row, st_in, pl.BlockSpec((SUBLANES, D_MODEL), lambda i: (0, 0))],
        out_specs=[row, st_out],
        out_shape=[jax.ShapeDtypeStruct((n, D_MODEL), BF16),
                   jax.ShapeDtypeStruct((1, n_st, n, D_MODEL), F32)],
        compiler_params=_params("arbitrary"),
        name="conv_c_sample",
    )(bg, cx, _rows_major(state), _pad_rows(w_conv, SUBLANES))
    return out, _rows_major(ns)


def _sg_kernel(up_ref, vp_ref, us_ref, vs_ref, lng_ref, lnb_ref, ws_ref, bs_ref, ws0_ref, bs0_ref,
               wo_ref, xp_ref, xs_ref, op_ref, os_ref, vns_ref, vn_ref, gt_ref, wsm_ref):
    i = pl.program_id(0)
    tm = up_ref.shape[0]

    @pl.when(i == 0)
    def _():
        row = lax.broadcasted_iota(jnp.int32, (CHUNK, CHUNK), 0)
        col = lax.broadcasted_iota(jnp.int32, (CHUNK, CHUNK), 1)
        for hd in range(N_SG_HEADS):
            wsm_ref[hd] = jnp.where(col <= row, ws_ref[hd], 0.0).astype(BF16)

    def norm(r0):
        vn_ref[pl.ds(r0, NORM_ROWS), :] = _ln_rows(vp_ref[pl.ds(r0, NORM_ROWS), :], lng_ref[...],
                                                   lnb_ref[...]).astype(BF16)
    _for_chunks(tm, NORM_ROWS, norm, unroll=NORM_UNROLL)

    for r0 in range(0, tm, CHUNK):
        for hd in range(N_SG_HEADS):
            lanes = pl.ds(hd * SG_HEAD_DIM, SG_HEAD_DIM)
            mixed = jnp.dot(wsm_ref[hd], vn_ref[pl.ds(r0, CHUNK), lanes],
                            preferred_element_type=F32) + bs_ref[:, lanes]
            gt_ref[pl.ds(r0, CHUNK), lanes] = (up_ref[pl.ds(r0, CHUNK), lanes].astype(F32)
                                               * mixed).astype(BF16)

    op_ref[...] = xp_ref[...] + jnp.dot(gt_ref[...], wo_ref[...], preferred_element_type=F32)

    @pl.when(i == pl.num_programs(0) - 1)
    def _():
        vn = _ln_rows(vs_ref[...], lng_ref[...], lnb_ref[...])
        vns_ref[...] = vn
        mixed = ws0_ref[...] * vn + bs0_ref[...]
        gated = (us_ref[...].astype(F32) * mixed).astype(BF16)
        os_ref[...] = xs_ref[...] + jnp.dot(gated, wo_ref[...], preferred_element_type=F32)


def _sg(up, vp, us, vs, ln_g, ln_b, w_s, layer, bs_map, ws0_map, bs0_map, w_o, xp, xs):
    mp, ms = up.shape[0], us.shape[0]
    vspec = pl.BlockSpec((1, D_MODEL), lambda i: (0, 0))
    rowp = pl.BlockSpec((TM_FULL, D_MODEL), lambda i: (i, 0))
    rows = pl.BlockSpec((ms, D_MODEL), lambda i: (0, 0))
    return pl.pallas_call(
        _sg_kernel,
        grid=(mp // TM_FULL,),
        in_specs=[rowp, rowp, rows, rows, vspec, vspec,
                  pl.BlockSpec((None, N_SG_HEADS, CHUNK, CHUNK), lambda i: (layer, 0, 0, 0)),
                  pl.BlockSpec((CHUNK, D_MODEL), lambda i: (0, 0)),
                  vspec, vspec,
                  pl.BlockSpec((D_MODEL, D_MODEL), lambda i: (0, 0), pipeline_mode=pl.Buffered(1)),
                  rowp, rows],
        out_specs=[rowp, rows, rows],
        out_shape=[jax.ShapeDtypeStruct((mp, D_MODEL), F32),
                   jax.ShapeDtypeStruct((ms, D_MODEL), F32),
                   jax.ShapeDtypeStruct((ms, D_MODEL), F32)],
        scratch_shapes=[pltpu.VMEM((TM_FULL, D_MODEL), BF16), pltpu.VMEM((TM_FULL, D_MODEL), BF16),
                        pltpu.VMEM((N_SG_HEADS, CHUNK, CHUNK), BF16)],
        compiler_params=_params("arbitrary"),
        name="sg_mix",
    )(up, vp, us, vs, _vec(ln_g), _vec(ln_b), w_s, bs_map, ws0_map, bs0_map, w_o, xp, xs)


def kernel(x_prompt, x_sample, state_conv_a, state_pool, state_short_conv, norm_mix, norm_mlp, norm_final, a_w_pw1, a_b_pw1, a_w_dw, a_b_dw, a_ln_g, a_ln_b, a_w_pw2, b_w_grp, b_scale, c_w_in, c_w_conv, c_w_out, d_w_uv, d_ln_g, d_ln_b, d_w_s, d_b_s, d_w_o, mlp_w1, mlp_w2):
    b, l, _ = x_prompt.shape
    mp = b * l
    ms = x_sample.shape[0]
    assert x_sample.shape[1] == 1 and mp % TM == 0 and l % TT == 0 and l % TT_POOL == 0
    flat3 = lambda a: a.reshape(b, l, D_MODEL)
    xp = x_prompt.reshape(mp, D_MODEL)
    xs = x_sample.reshape(ms, D_MODEL)
    tf = 512

    (gp,), (gs,), (w_pw2,) = _mm_parts(
        "a_pw1_glu", xp, xs, norm_mix[0], a_w_pw1.astype(BF16), 0, a_b_pw1[0], 2, (F32,),
        _glu_epilogue, 1024, cast_jobs=[(a_w_pw2, 0)])
    cp3, (w1_0, w2_0) = _conv_a_prompt(flat3(gp), a_w_dw[0], a_b_dw[0], a_ln_g[0], a_ln_b[0],
                                       cast_jobs=[(mlp_w1, 0), (mlp_w2, 0)])
    cp = cp3.reshape(mp, D_MODEL)
    cs, sa = _conv_a_sample(state_conv_a, 0, gs, a_w_dw[0], a_b_dw[0], a_ln_g[0], a_ln_b[0], 32)
    pa = flat3(gp)[:, l - (CONV_A_WIDTH - 1):]
    xp, xs = _mm_res("a_pw2", cp, cs, w_pw2, xp, xs)
    xp, xs, (w1_1, w2_1, w_in, w_out) = _mlp(
        "mlp0", xp, xs, norm_mlp[0], w1_0, w2_0, None, tf,
        cast_jobs=[(mlp_w1, 1), (mlp_w2, 1), (c_w_in, 0), (c_w_out, 0)])

    xp3, hs = _pool_prompt(flat3(xp), norm_mix[1], b_w_grp, 0, b_scale[0])
    xp = xp3.reshape(mp, D_MODEL)
    pb = hs[:, -1, POOL_HALO - POOL_STATE:]
    xs, sb = _pool_sample(xs, state_pool, norm_mix[1], b_w_grp, 0, b_scale[0], 32)
    xp, xs, (w1_2, w2_2, w_uv, w_o) = _mlp(
        "mlp1", xp, xs, norm_mlp[1], w1_1, w2_1, None, tf,
        cast_jobs=[(mlp_w1, 2), (mlp_w2, 2), (d_w_uv, 0), (d_w_o, 0)])

    zp, cx_tail, bgs, cxs = _c_in_conv(xp, xs, norm_mix[2], w_in, c_w_conv[0], l, 512)
    zs, sc = _conv_c_sample(bgs, cxs, state_short_conv, 0, c_w_conv[0])
    pc = cx_tail.reshape(b, l // TM, SUBLANES, D_MODEL)[:, -1, SUBLANES - (CONV_C_WIDTH - 1):]
    xp, xs = _mm_res("c_out", zp, zs, w_out, xp, xs)
    xp, xs, (w1_3, w2_3) = _mlp("mlp2", xp, xs, norm_mlp[2], w1_2, w2_2, None, tf,
                                cast_jobs=[(mlp_w1, 3), (mlp_w2, 3)])

    (up, vp), (us, vs), _ = _mm_parts("d_uv_gelu", xp, xs, norm_mix[3], w_uv, None, None, 2,
                                      (BF16, F32), _gelu_epilogue, 1024)
    bs_map = jnp.repeat(d_b_s[0].T, SG_HEAD_DIM, axis=1)
    ws0_map = _vec(jnp.repeat(d_w_s[0, :, 0, 0], SG_HEAD_DIM))
    bs0_map = _vec(jnp.repeat(d_b_s[0, :, 0], SG_HEAD_DIM))
    xp, xs, sd = _sg(up, vp, us, vs, d_ln_g[0], d_ln_b[0], d_w_s, 0, bs_map, ws0_map, bs0_map,
                     w_o, xp, xs)
    yp, ys, _ = _mlp("mlp3", xp, xs, norm_mlp[3], w1_3, w2_3, None, tf, final_g=norm_final)

    return (yp.reshape(b, l, D_MODEL), ys.reshape(ms, 1, D_MODEL),
            pa[None], sa, pb[None], sb, pc[None], sc,
            sd.reshape(1, ms, 1, D_MODEL))
```

```python
import functools

import jax
import jax.numpy as jnp
from jax import lax
from jax.experimental import pallas as pl
from jax.experimental.pallas import tpu as pltpu

D_MODEL = 2048
D_FF = 4 * D_MODEL
CONV_A_WIDTH = 31
POOL_WINDOWS = (2, 4, 8, 16)
POOL_GROUP = D_MODEL // len(POOL_WINDOWS)
POOL_STATE = max(POOL_WINDOWS) - 1
CONV_C_WIDTH = 3
CHUNK = 128
N_SG_HEADS = 8
SG_HEAD_DIM = D_MODEL // N_SG_HEADS
PAST_LEN = 16384
RMS_EPS = 1e-6
LN_EPS = 1e-5

SUBLANES = 8
BF16_ROWS = 16
VMEM_LIMIT_BYTES = 60000 * 1024
HALO = 32

TM = 1024
TM_FULL = 512
TT = 512
TB_STATE = 16
TT_POOL = 512
NORM_ROWS = 32
NORM_UNROLL = 4

F32 = jnp.float32
BF16 = jnp.bfloat16


def _params(*sem):
    return pltpu.CompilerParams(dimension_semantics=sem, vmem_limit_bytes=VMEM_LIMIT_BYTES)


def _for_chunks(n, chunk, body, unroll=1):
    assert n % chunk == 0
    if n == chunk:
        body(0)
    else:
        def step(i, c):
            body(pl.multiple_of(i * chunk, chunk))
            return c
        lax.fori_loop(0, n // chunk, step, 0, unroll=unroll)


def _rms_rows(x, g):
    r = lax.rsqrt(jnp.mean(x * x, axis=-1, keepdims=True) + RMS_EPS)
    return x * r * g


def _ln_rows(x, g, b):
    mu = jnp.mean(x, axis=-1, keepdims=True)
    xc = x - mu
    var = jnp.mean(xc * xc, axis=-1, keepdims=True)
    return xc * lax.rsqrt(var + LN_EPS) * g + b


def _row_chunk(tm):
    for c in (64, 32, 16, 8):
        if tm % c == 0:
            return c
    raise ValueError(tm)


def _rms_to_bf16(x_ref, g_ref, xn_ref):
    n = x_ref.shape[0]
    rc = _row_chunk(n)

    def body(r0):
        xn_ref[pl.ds(r0, rc), :] = _rms_rows(x_ref[pl.ds(r0, rc), :], g_ref[...]).astype(BF16)
    _for_chunks(n, rc, body, unroll=2)


def _vec(v):
    return v.reshape(1, D_MODEL)


def _sample_col(i, j, ni):
    return jnp.where(i == ni - 1, j, 0)


def _wspec(block, index, layer, **kw):
    if layer is None:
        return pl.BlockSpec(block, index, **kw)
    return pl.BlockSpec((None,) + block, lambda *g: (layer,) + tuple(index(*g)), **kw)


def _cast_job_specs(jobs, n_steps, step_of):
    in_specs, args, out_specs, out_shapes = [], [], [], []
    for src, layer in jobs:
        _, r, c = src.shape
        assert r % (n_steps * BF16_ROWS) == 0, (src.shape, n_steps)
        pr = r // n_steps
        in_specs.append(pl.BlockSpec((None, pr, c),
                                     lambda *g, layer=layer: (layer, step_of(*g), 0)))
        args.append(src)
        out_specs.append(pl.BlockSpec((pr, c), lambda *g: (step_of(*g), 0)))
        out_shapes.append(jax.ShapeDtypeStruct((r, c), BF16))
    return in_specs, args, out_specs, out_shapes


def _run_cast_jobs(src_refs, dst_refs):
    for s_ref, d_ref in zip(src_refs, dst_refs):
        d_ref[...] = s_ref[...].astype(BF16)


def _mm_parts_kernel(*refs, n_parts, n_out, has_bias, n_jobs, epilogue):
    xp_ref, xs_ref, g_ref = refs[:3]
    w_refs = refs[3:3 + n_parts]
    pos = 3 + n_parts
    b_refs = refs[pos:pos + n_parts] if has_bias else ()
    pos += n_parts if has_bias else 0
    job_src = refs[pos:pos + n_jobs]
    pos += n_jobs
    op_refs = refs[pos:pos + n_out]
    os_refs = refs[pos + n_out:pos + 2 * n_out]
    pos += 2 * n_out
    job_dst = refs[pos:pos + n_jobs]
    xnp_ref, xns_ref = refs[pos + n_jobs:]
    i, j = pl.program_id(0), pl.program_id(1)
    last_i = pl.num_programs(0) - 1

    @pl.when(j == 0)
    def _():
        _rms_to_bf16(xp_ref, g_ref, xnp_ref)

    @pl.when((j == 0) & (i == last_i))
    def _():
        _rms_to_bf16(xs_ref, g_ref, xns_ref)

    def rows(xn_ref, o_refs):
        parts = []
        for p in range(n_parts):
            a = jnp.dot(xn_ref[...], w_refs[p][...], preferred_element_type=F32)
            if has_bias:
                a = a + b_refs[p][...]
            parts.append(a)
        for o_ref, o in zip(o_refs, epilogue(*parts)):
            o_ref[...] = o.astype(o_ref.dtype)

    rows(xnp_ref, op_refs)
    _run_cast_jobs(job_src, job_dst)

    @pl.when(i == last_i)
    def _():
        rows(xns_ref, os_refs)


def _mm_parts(name, xp, xs, g, w, layer, bias, n_parts, out_dtypes, epilogue, tn, cast_jobs=()):
    mp, ms = xp.shape[0], xs.shape[0]
    nj = D_MODEL // tn
    ni = mp // TM
    in_specs = [pl.BlockSpec((TM, D_MODEL), lambda i, j: (i, 0)),
                pl.BlockSpec((ms, D_MODEL), lambda i, j: (0, 0)),
                pl.BlockSpec((1, D_MODEL), lambda i, j: (0, 0))]
    args = [xp, xs, _vec(g)]
    for p in range(n_parts):
        in_specs.append(_wspec((D_MODEL, tn), lambda i, j, p=p: (0, p * nj + j), layer))
        args.append(w)
    if bias is not None:
        for p in range(n_parts):
            in_specs.append(pl.BlockSpec((1, tn), lambda i, j, p=p: (0, p * nj + j)))
            args.append(bias.reshape(1, n_parts * D_MODEL))
    j_in, j_args, j_out, j_shapes = _cast_job_specs(cast_jobs, ni * nj, lambda i, j: i * nj + j)
    kern = functools.partial(_mm_parts_kernel, n_parts=n_parts, n_out=len(out_dtypes),
                             has_bias=bias is not None, n_jobs=len(cast_jobs), epilogue=epilogue)
    outs = pl.pallas_call(
        kern,
        grid=(ni, nj),
        in_specs=in_specs + j_in,
        out_specs=([pl.BlockSpec((TM, tn), lambda i, j: (i, j)) for _ in out_dtypes]
                   + [pl.BlockSpec((ms, tn), lambda i, j: (0, _sample_col(i, j, ni)))
                      for _ in out_dtypes] + j_out),
        out_shape=([jax.ShapeDtypeStruct((mp, D_MODEL), dt) for dt in out_dtypes]
                   + [jax.ShapeDtypeStruct((ms, D_MODEL), dt) for dt in out_dtypes] + j_shapes),
        scratch_shapes=[pltpu.VMEM((TM, D_MODEL), BF16), pltpu.VMEM((ms, D_MODEL), BF16)],
        compiler_params=_params("arbitrary", "arbitrary"),
        name=name,
    )(*args, *j_args)
    n = len(out_dtypes)
    return outs[:n], outs[n:2 * n], outs[2 * n:]


def _glu_epilogue(a, b):
    return (a * jax.nn.sigmoid(b),)


def _gelu_epilogue(a, b):
    return (jax.nn.gelu(a), jax.nn.gelu(b))


def _mm_res_kernel(ap_ref, as_ref, w_ref, xp_ref, xs_ref, op_ref, os_ref):
    op_ref[...] = xp_ref[...] + jnp.dot(ap_ref[...], w_ref[...], preferred_element_type=F32)

    @pl.when(pl.program_id(0) == pl.num_programs(0) - 1)
    def _():
        os_ref[...] = xs_ref[...] + jnp.dot(as_ref[...], w_ref[...], preferred_element_type=F32)


def _mm_res(name, ap, as_, w, xp, xs):
    mp, ms = ap.shape[0], as_.shape[0]
    rowp = pl.BlockSpec((TM_FULL, D_MODEL), lambda i: (i, 0))
    rows = pl.BlockSpec((ms, D_MODEL), lambda i: (0, 0))
    return pl.pallas_call(
        _mm_res_kernel,
        grid=(mp // TM_FULL,),
        in_specs=[rowp, rows,
                  pl.BlockSpec((D_MODEL, D_MODEL), lambda i: (0, 0), pipeline_mode=pl.Buffered(1)),
                  rowp, rows],
        out_specs=[rowp, rows],
        out_shape=[jax.ShapeDtypeStruct((mp, D_MODEL), F32),
                   jax.ShapeDtypeStruct((ms, D_MODEL), F32)],
        compiler_params=_params("arbitrary"),
        name=name,
    )(ap, as_, w, xp, xs)


def _mlp_kernel(*refs, final, n_jobs):
    n_in = 6 if final else 5
    xp_ref, xs_ref, g_ref, w1_ref, w2_ref = refs[:5]
    gf_ref = refs[5] if final else None
    job_src = refs[n_in:n_in + n_jobs]
    op_ref, os_ref = refs[n_in + n_jobs:n_in + n_jobs + 2]
    job_dst = refs[n_in + n_jobs + 2:n_in + 2 * n_jobs + 2]
    (xn_ref,) = refs[n_in + 2 * n_jobs + 2:]
    i, j = pl.program_id(0), pl.program_id(1)
    last_i = pl.num_programs(0) - 1
    last_j = pl.num_programs(1) - 1
    tm, ms = xp_ref.shape[0], xs_ref.shape[0]

    @pl.when(j == 0)
    def _():
        _rms_to_bf16(xp_ref, g_ref, xn_ref.at[pl.ds(0, tm)])

    @pl.when((j == 0) & (i == last_i))
    def _():
        _rms_to_bf16(xs_ref, g_ref, xn_ref.at[pl.ds(tm, ms)])

    def step(with_sample, first):
        xn = xn_ref[...] if with_sample else xn_ref[pl.ds(0, tm), :]
        h = jnp.dot(xn, w1_ref[...], preferred_element_type=F32)
        h = jnp.square(jnp.maximum(h, 0.0)).astype(BF16)
        y = jnp.dot(h, w2_ref[...], preferred_element_type=F32)
        outs = [(xp_ref, op_ref, y[:tm] if with_sample else y)]
        if with_sample:
            outs.append((xs_ref, os_ref, y[tm:]))
        for x_ref, o_ref, yy in outs:
            if first:
                o_ref[...] = x_ref[...] + yy
            else:
                o_ref[...] += yy
        _run_cast_jobs(job_src, job_dst)

    for with_sample in (False, True):
        for first in (True, False):
            cond = ((i == last_i) if with_sample else (i != last_i)) & ((j == 0) if first else (j > 0))
            pl.when(cond)(functools.partial(step, with_sample, first))

    if final:
        def norm(o_ref):
            n = o_ref.shape[0]
            rc = _row_chunk(n)

            def body(r0):
                o_ref[pl.ds(r0, rc), :] = _rms_rows(o_ref[pl.ds(r0, rc), :], gf_ref[...])
            _for_chunks(n, rc, body, unroll=2)

        @pl.when(j == last_j)
        def _():
            norm(op_ref)

        @pl.when((j == last_j) & (i == last_i))
        def _():
            norm(os_ref)


def _mlp(name, xp, xs, g, w1, w2, layer, tf, final_g=None, cast_jobs=()):
    mp, ms = xp.shape[0], xs.shape[0]
    final = final_g is not None
    ni, nj = mp // TM, D_FF // tf
    in_specs = [pl.BlockSpec((TM, D_MODEL), lambda i, j: (i, 0)),
                pl.BlockSpec((ms, D_MODEL), lambda i, j: (0, 0)),
                pl.BlockSpec((1, D_MODEL), lambda i, j: (0, 0)),
                _wspec((D_MODEL, tf), lambda i, j: (0, j), layer),
                _wspec((tf, D_MODEL), lambda i, j: (j, 0), layer)]
    args = [xp, xs, _vec(g), w1, w2]
    if final:
        in_specs.append(pl.BlockSpec((1, D_MODEL), lambda i, j: (0, 0)))
        args.append(_vec(final_g))
    j_in, j_args, j_out, j_shapes = _cast_job_specs(cast_jobs, ni * nj, lambda i, j: i * nj + j)
    outs = pl.pallas_call(
        functools.partial(_mlp_kernel, final=final, n_jobs=len(cast_jobs)),
        grid=(ni, nj),
        in_specs=in_specs + j_in,
        out_specs=[pl.BlockSpec((TM, D_MODEL), lambda i, j: (i, 0)),
                   pl.BlockSpec((ms, D_MODEL), lambda i, j: (0, 0))] + j_out,
        out_shape=[jax.ShapeDtypeStruct((mp, D_MODEL), F32),
                   jax.ShapeDtypeStruct((ms, D_MODEL), F32)] + j_shapes,
        scratch_shapes=[pltpu.VMEM((TM + ms, D_MODEL), BF16)],
        compiler_params=_params("arbitrary", "arbitrary"),
        name=name,
    )(*args, *j_args)
    return outs[0], outs[1], outs[2:]


CONV_ROWS = 128
CONV_LANES = 256


def _carry_halo(ext_ref, t, tt, halo):
    @pl.when(t == 0)
    def _():
        ext_ref[pl.ds(0, halo), :] = jnp.zeros((halo, D_MODEL), F32)

    @pl.when(t > 0)
    def _():
        ext_ref[pl.ds(0, halo), :] = ext_ref[pl.ds(tt, halo), :]


def _conv_a_kernel(*refs, tt, n_jobs):
    g_ref, w_ref, bdw_ref, lng_ref, lnb_ref = refs[:5]
    job_src = refs[5:5 + n_jobs]
    o_ref = refs[5 + n_jobs]
    job_dst = refs[6 + n_jobs:6 + 2 * n_jobs]
    ext_ref, c_ref = refs[6 + 2 * n_jobs:]
    _run_cast_jobs(job_src, job_dst)
    _carry_halo(ext_ref, pl.program_id(1), tt, HALO)

    def copy_in(r0):
        ext_ref[pl.ds(HALO + r0, 64), :] = g_ref[0, pl.ds(r0, 64), :]
    _for_chunks(tt, 64, copy_in)

    off = HALO - (CONV_A_WIDTH - 1)
    win = CONV_ROWS + HALO
    tiles = CONV_ROWS // SUBLANES
    for c0 in range(0, D_MODEL, CONV_LANES):
        lanes = pl.ds(c0, CONV_LANES)

        def rows(r0, lanes=lanes):
            w = ext_ref[pl.ds(r0, win), lanes]
            acc = jnp.zeros((tiles, SUBLANES, CONV_LANES), F32)
            for s in range(SUBLANES):
                ws = w if s == 0 else pltpu.roll(w, win - s, 0)
                for k in range(CONV_A_WIDTH):
                    if (off + k) % SUBLANES == s:
                        a = off + k - s
                        tap = ws[a:a + CONV_ROWS].reshape(tiles, SUBLANES, CONV_LANES)
                        acc = acc + tap * w_ref[k, :, lanes][None]
            c_ref[pl.ds(r0, CONV_ROWS), lanes] = acc.reshape(CONV_ROWS, CONV_LANES)
        _for_chunks(tt, CONV_ROWS, rows)

    def norm(r0):
        c = c_ref[pl.ds(r0, NORM_ROWS), :] + bdw_ref[...]
        y = _ln_rows(c, lng_ref[...], lnb_ref[...])
        o_ref[0, pl.ds(r0, NORM_ROWS), :] = (y * jax.nn.sigmoid(y)).astype(BF16)
    _for_chunks(tt, NORM_ROWS, norm, unroll=NORM_UNROLL)


def _pad_rows(w, rows):
    return jnp.zeros((rows, D_MODEL), F32).at[:w.shape[0]].set(w)


def _conv_a_prompt(g3, w_dw, b_dw, ln_g, ln_b, cast_jobs=()):
    b, l, _ = g3.shape
    nt = l // TT
    vspec = pl.BlockSpec((1, D_MODEL), lambda i, t: (0, 0))
    w_rep = jnp.broadcast_to(_pad_rows(w_dw, 32)[:, None, :], (32, SUBLANES, D_MODEL))
    j_in, j_args, j_out, j_shapes = _cast_job_specs(cast_jobs, b * nt, lambda i, t: i * nt + t)
    outs = pl.pallas_call(
        functools.partial(_conv_a_kernel, tt=TT, n_jobs=len(cast_jobs)),
        grid=(b, nt),
        in_specs=[pl.BlockSpec((1, TT, D_MODEL), lambda i, t: (i, t, 0)),
                  pl.BlockSpec((32, SUBLANES, D_MODEL), lambda i, t: (0, 0, 0)),
                  vspec, vspec, vspec] + j_in,
        out_specs=[pl.BlockSpec((1, TT, D_MODEL), lambda i, t: (i, t, 0))] + j_out,
        out_shape=[jax.ShapeDtypeStruct((b, l, D_MODEL), BF16)] + j_shapes,
        scratch_shapes=[pltpu.VMEM((TT + HALO, D_MODEL), F32),
                        pltpu.VMEM((TT, D_MODEL), F32)],
        compiler_params=_params("arbitrary", "arbitrary"),
        name="conv_a_prompt",
    )(g3, w_rep, _vec(b_dw), _vec(ln_g), _vec(ln_b), *j_args)
    return outs[0], outs[1:]


def _rows_major(state):
    return jnp.transpose(state, (0, 2, 1, 3))


def _push_state(st_ref, new_row, ns_ref):
    n_st = st_ref.shape[0]
    for k in range(n_st - 1):
        ns_ref[k] = st_ref[k + 1]
    ns_ref[n_st - 1] = new_row


def _state_specs(n_st, tb, layer):
    return (pl.BlockSpec((None, n_st, tb, D_MODEL), lambda i: (layer, 0, i, 0)),
            pl.BlockSpec((None, n_st, tb, D_MODEL), lambda i: (0, 0, i, 0)))


def _conv_a_sample_kernel(st_ref, g_ref, w_ref, bdw_ref, lng_ref, lnb_ref, o_ref, ns_ref):
    n_st = CONV_A_WIDTH - 1
    acc = g_ref[...] * w_ref[pl.ds(n_st, 1), :]
    for k in range(n_st):
        acc = acc + st_ref[k] * w_ref[pl.ds(k, 1), :]
    y = _ln_rows(acc + bdw_ref[...], lng_ref[...], lnb_ref[...])
    o_ref[...] = (y * jax.nn.sigmoid(y)).astype(BF16)
    _push_state(st_ref, g_ref[...], ns_ref)


def _conv_a_sample(state, layer, g, w_dw, b_dw, ln_g, ln_b, tb):
    n = g.shape[0]
    n_st = CONV_A_WIDTH - 1
    vspec = pl.BlockSpec((1, D_MODEL), lambda i: (0, 0))
    st_in, st_out = _state_specs(n_st, tb, layer)
    out, ns = pl.pallas_call(
        _conv_a_sample_kernel,
        grid=(n // tb,),
        in_specs=[st_in,
                  pl.BlockSpec((tb, D_MODEL), lambda i: (i, 0)),
                  pl.BlockSpec((32, D_MODEL), lambda i: (0, 0)),
                  vspec, vspec, vspec],
        out_specs=[pl.BlockSpec((tb, D_MODEL), lambda i: (i, 0)), st_out],
        out_shape=[jax.ShapeDtypeStruct((n, D_MODEL), BF16),
                   jax.ShapeDtypeStruct((1, n_st, n, D_MODEL), F32)],
        compiler_params=_params("arbitrary"),
        name="conv_a_sample",
    )(_rows_major(state), g, _pad_rows(w_dw, 32), _vec(b_dw), _vec(ln_g), _vec(ln_b))
    return out, _rows_major(ns)


POOL_HALO = 16


def _pool_prompt_kernel(x_ref, g_ref, w_ref, sc_ref, o_ref, hs_ref, ext_ref, diff_ref, wb_ref,
                        *, tt):
    t = pl.program_id(1)
    H = POOL_HALO
    _carry_halo(ext_ref, t, tt, H)

    def norm(r0):
        ext_ref[pl.ds(H + r0, NORM_ROWS), :] = _rms_rows(x_ref[0, pl.ds(r0, NORM_ROWS), :],
                                                         g_ref[...])
    _for_chunks(tt, NORM_ROWS, norm, unroll=NORM_UNROLL)

    hs_ref[0, 0] = ext_ref[pl.ds(tt, H), :]

    @pl.when((pl.program_id(0) == 0) & (t == 0))
    def _():
        wb_ref[...] = w_ref[...].astype(BF16)

    pr = 64
    win = pr + H
    for gi, w in enumerate(POOL_WINDOWS):
        lanes = pl.ds(gi * POOL_GROUP, POOL_GROUP)

        def rows(r0, w=w, lanes=lanes):
            s = ext_ref[pl.ds(r0, win), lanes]
            cur = s[H:]
            d = 1
            while d < w:
                s = s + pltpu.roll(s, d, 0)
                d *= 2
            pos = t * tt + r0 + lax.broadcasted_iota(jnp.int32, (pr, 1), 0)
            cnt = jnp.minimum(pos + 1, w).astype(F32)
            diff_ref[pl.ds(r0, pr), lanes] = (s[H:] / cnt - cur).astype(BF16)
        _for_chunks(tt, pr, rows)

    for gi in range(len(POOL_WINDOWS)):
        sl = slice(gi * POOL_GROUP, (gi + 1) * POOL_GROUP)
        y = jnp.dot(diff_ref[:, sl], wb_ref[gi], preferred_element_type=F32)
        o_ref[0, :, sl] = x_ref[0, :, sl] + y * sc_ref[:, sl]


def _pool_prompt(x3, g, w_grp, layer, scale):
    b, l, _ = x3.shape
    tt = TT_POOL
    nt = l // tt
    ng = len(POOL_WINDOWS)
    vspec = pl.BlockSpec((1, D_MODEL), lambda i, t: (0, 0))
    return pl.pallas_call(
        functools.partial(_pool_prompt_kernel, tt=tt),
        grid=(b, nt),
        in_specs=[pl.BlockSpec((1, tt, D_MODEL), lambda i, t: (i, t, 0)),
                  vspec,
                  pl.BlockSpec((None, ng, POOL_GROUP, POOL_GROUP), lambda i, t: (layer, 0, 0, 0)),
                  vspec],
        out_specs=[pl.BlockSpec((1, tt, D_MODEL), lambda i, t: (i, t, 0)),
                   pl.BlockSpec((1, 1, POOL_HALO, D_MODEL), lambda i, t: (i, t, 0, 0))],
        out_shape=[jax.ShapeDtypeStruct((b, l, D_MODEL), F32),
                   jax.ShapeDtypeStruct((b, nt, POOL_HALO, D_MODEL), F32)],
        scratch_shapes=[pltpu.VMEM((tt + POOL_HALO, D_MODEL), F32),
                        pltpu.VMEM((tt, D_MODEL), BF16),
                        pltpu.VMEM((ng, POOL_GROUP, POOL_GROUP), BF16)],
        compiler_params=_params("arbitrary", "arbitrary"),
        name="pool_prompt",
    )(x3, _vec(g), w_grp, _vec(scale))


def _pool_sample_kernel(x_ref, st_ref, g_ref, w_ref, sc_ref, o_ref, ns_ref):
    x = x_ref[...]
    h = _rms_rows(x, g_ref[...])
    for gi, w in enumerate(POOL_WINDOWS):
        sl = slice(gi * POOL_GROUP, (gi + 1) * POOL_GROUP)
        cur = h[:, sl]
        s = cur
        for d in range(1, w):
            s = s + st_ref[POOL_STATE - d, :, sl]
        diff = s / float(min(PAST_LEN + 1, w)) - cur
        y = jnp.dot(diff.astype(BF16), w_ref[gi].astype(BF16), preferred_element_type=F32)
        o_ref[:, sl] = x[:, sl] + y * sc_ref[:, sl]
    _push_state(st_ref, h, ns_ref)


def _pool_sample(x, state, g, w_grp, layer, scale, tb):
    n = x.shape[0]
    ng = len(POOL_WINDOWS)
    vspec = pl.BlockSpec((1, D_MODEL), lambda i: (0, 0))
    row = pl.BlockSpec((tb, D_MODEL), lambda i: (i, 0))
    st_in, st_out = _state_specs(POOL_STATE, tb, layer)
    out, ns = pl.pallas_call(
        _pool_sample_kernel,
        grid=(n // tb,),
        in_specs=[row, st_in, vspec,
                  pl.BlockSpec((None, ng, POOL_GROUP, POOL_GROUP), lambda i: (layer, 0, 0, 0)),
                  vspec],
        out_specs=[row, st_out],
        out_shape=[jax.ShapeDtypeStruct((n, D_MODEL), F32),
                   jax.ShapeDtypeStruct((1, POOL_STATE, n, D_MODEL), F32)],
        compiler_params=_params("arbitrary"),
        name="pool_sample",
    )(x, _rows_major(state), _vec(g), w_grp, _vec(scale))
    return out, _rows_major(ns)


C_ROWS = 128


def _c_in_kernel(xp_ref, xs_ref, g_ref, wb_ref, wc_ref, wx_ref, wconv_ref,
                 z_ref, tail_ref, bgs_ref, cxs_ref, xnp_ref, xns_ref, ext_ref, carry_ref,
                 *, blocks_per_seq):
    i, j = pl.program_id(0), pl.program_id(1)
    last_i = pl.num_programs(0) - 1
    H = SUBLANES
    tm = xp_ref.shape[0]

    @pl.when(j == 0)
    def _():
        _rms_to_bf16(xp_ref, g_ref, xnp_ref)

    @pl.when((j == 0) & (i == last_i))
    def _():
        _rms_to_bf16(xs_ref, g_ref, xns_ref)

    def parts(xn_ref):
        xn = xn_ref[...]
        return tuple(jnp.dot(xn, w_ref[...], preferred_element_type=F32)
                     for w_ref in (wb_ref, wc_ref, wx_ref))

    bg, c, xv = parts(xnp_ref)
    cx = c * xv
    prev = carry_ref[j]
    ext_ref[pl.ds(0, H), :] = jnp.where(i % blocks_per_seq == 0, jnp.zeros_like(prev), prev)
    ext_ref[pl.ds(H, tm), :] = cx
    carry_ref[j] = cx[tm - H:]
    tail_ref[0] = cx[tm - H:]
    win = C_ROWS + H
    for r0 in range(0, tm, C_ROWS):
        w = ext_ref[pl.ds(r0, win), :]
        y = (pltpu.roll(w, 2, 0)[H:] * wconv_ref[pl.ds(0, 1), :]
             + pltpu.roll(w, 1, 0)[H:] * wconv_ref[pl.ds(1, 1), :]
             + w[H:] * wconv_ref[pl.ds(2, 1), :])
        z_ref[pl.ds(r0, C_ROWS), :] = (bg[r0:r0 + C_ROWS] * y).astype(BF16)

    @pl.when(i == last_i)
    def _():
        bs, cs, xvs = parts(xns_ref)
        bgs_ref[...] = bs
        cxs_ref[...] = cs * xvs


def _c_in_conv(xp, xs, g, w, w_conv, seq_len, tn):
    mp, ms = xp.shape[0], xs.shape[0]
    nj = D_MODEL // tn
    ni = mp // TM
    assert seq_len % TM == 0
    wspecs = [pl.BlockSpec((D_MODEL, tn), lambda i, j, p=p: (0, p * nj + j)) for p in range(3)]
    tile_s = pl.BlockSpec((ms, tn), lambda i, j: (0, _sample_col(i, j, ni)))
    return pl.pallas_call(
        functools.partial(_c_in_kernel, blocks_per_seq=seq_len // TM),
        grid=(ni, nj),
        in_specs=[pl.BlockSpec((TM, D_MODEL), lambda i, j: (i, 0)),
                  pl.BlockSpec((ms, D_MODEL), lambda i, j: (0, 0)),
                  pl.BlockSpec((1, D_MODEL), lambda i, j: (0, 0))] + wspecs
                 + [pl.BlockSpec((SUBLANES, tn), lambda i, j: (0, j))],
        out_specs=[pl.BlockSpec((TM, tn), lambda i, j: (i, j)),
                   pl.BlockSpec((1, SUBLANES, tn), lambda i, j: (i, 0, j)),
                   tile_s, tile_s],
        out_shape=[jax.ShapeDtypeStruct((mp, D_MODEL), BF16),
                   jax.ShapeDtypeStruct((ni, SUBLANES, D_MODEL), F32),
                   jax.ShapeDtypeStruct((ms, D_MODEL), F32),
                   jax.ShapeDtypeStruct((ms, D_MODEL), F32)],
        scratch_shapes=[pltpu.VMEM((TM, D_MODEL), BF16), pltpu.VMEM((ms, D_MODEL), BF16),
                        pltpu.VMEM((TM + SUBLANES, tn), F32),
                        pltpu.VMEM((nj, SUBLANES, tn), F32)],
        compiler_params=_params("arbitrary", "arbitrary"),
        name="c_in_conv",
    )(xp, xs, _vec(g), w, w, w, _pad_rows(w_conv, SUBLANES))


def _conv_c_sample_kernel(b_ref, cx_ref, st_ref, w_ref, o_ref, ns_ref):
    y = (st_ref[0] * w_ref[pl.ds(0, 1), :] + st_ref[1] * w_ref[pl.ds(1, 1), :]
         + cx_ref[...] * w_ref[pl.ds(2, 1), :])
    o_ref[...] = (b_ref[...] * y).astype(BF16)
    _push_state(st_ref, cx_ref[...], ns_ref)


def _conv_c_sample(bg, cx, state, layer, w_conv):
    n = bg.shape[0]
    n_st = CONV_C_WIDTH - 1
    row = pl.BlockSpec((n, D_MODEL), lambda i: (0, 0))
    st_in, st_out = _state_specs(n_st, n, layer)
    out, ns = pl.pallas_call(
        _conv_c_sample_kernel,
        grid=(1,),
        in_specs=[row, row, st_in, pl.BlockSpec((SUBLANES, D_MODEL), lambda i: (0, 0))],
        out_specs=[row, st_out],
        out_shape=[jax.ShapeDtypeStruct((n, D_MODEL), BF16),
                   jax.ShapeDtypeStruct((1, n_st, n, D_MODEL), F32)],
        compiler_params=_params("arbitrary"),
        name="conv_c_sample",
    )(bg, cx, _rows_major(state), _pad_rows(w_conv, SUBLANES))
    return out, _rows_major(ns)


def _sg_kernel(up_ref, vp_ref, us_ref, vs_ref, lng_ref, lnb_ref, ws_ref, bs_ref, ws0_ref, bs0_ref,
               wo_ref, xp_ref, xs_ref, op_ref, os_ref, vns_ref, vn_ref, gt_ref, wsm_ref):
    i = pl.program_id(0)
    tm = up_ref.shape[0]

    @pl.when(i == 0)
    def _():
        row = lax.broadcasted_iota(jnp.int32, (CHUNK, CHUNK), 0)
        col = lax.broadcasted_iota(jnp.int32, (CHUNK, CHUNK), 1)
        for hd in range(N_SG_HEADS):
            wsm_ref[hd] = jnp.where(col <= row, ws_ref[hd], 0.0).astype(BF16)

    def norm(r0):
        vn_ref[pl.ds(r0, NORM_ROWS), :] = _ln_rows(vp_ref[pl.ds(r0, NORM_ROWS), :], lng_ref[...],
                                                   lnb_ref[...]).astype(BF16)
    _for_chunks(tm, NORM_ROWS, norm, unroll=NORM_UNROLL)

    for r0 in range(0, tm, CHUNK):
        for hd in range(N_SG_HEADS):
            lanes = pl.ds(hd * SG_HEAD_DIM, SG_HEAD_DIM)
            mixed = jnp.dot(wsm_ref[hd], vn_ref[pl.ds(r0, CHUNK), lanes],
                            preferred_element_type=F32) + bs_ref[:, lanes]
            gt_ref[pl.ds(r0, CHUNK), lanes] = (up_ref[pl.ds(r0, CHUNK), lanes].astype(F32)
                                               * mixed).astype(BF16)

    op_ref[...] = xp_ref[...] + jnp.dot(gt_ref[...], wo_ref[...], preferred_element_type=F32)

    @pl.when(i == pl.num_programs(0) - 1)
    def _():
        vn = _ln_rows(vs_ref[...], lng_ref[...], lnb_ref[...])
        vns_ref[...] = vn
        mixed = ws0_ref[...] * vn + bs0_ref[...]
        gated = (us_ref[...].astype(F32) * mixed).astype(BF16)
        os_ref[...] = xs_ref[...] + jnp.dot(gated, wo_ref[...], preferred_element_type=F32)


def _sg(up, vp, us, vs, ln_g, ln_b, w_s, layer, bs_map, ws0_map, bs0_map, w_o, xp, xs):
    mp, ms = up.shape[0], us.shape[0]
    vspec = pl.BlockSpec((1, D_MODEL), lambda i: (0, 0))
    rowp = pl.BlockSpec((TM_FULL, D_MODEL), lambda i: (i, 0))
    rows = pl.BlockSpec((ms, D_MODEL), lambda i: (0, 0))
    return pl.pallas_call(
        _sg_kernel,
        grid=(mp // TM_FULL,),
        in_specs=[rowp, rowp, rows, rows, vspec, vspec,
                  pl.BlockSpec((None, N_SG_HEADS, CHUNK, CHUNK), lambda i: (layer, 0, 0, 0)),
                  pl.BlockSpec((CHUNK, D_MODEL), lambda i: (0, 0)),
                  vspec, vspec,
                  pl.BlockSpec((D_MODEL, D_MODEL), lambda i: (0, 0), pipeline_mode=pl.Buffered(1)),
                  rowp, rows],
        out_specs=[rowp, rows, rows],
        out_shape=[jax.ShapeDtypeStruct((mp, D_MODEL), F32),
                   jax.ShapeDtypeStruct((ms, D_MODEL), F32),
                   jax.ShapeDtypeStruct((ms, D_MODEL), F32)],
        scratch_shapes=[pltpu.VMEM((TM_FULL, D_MODEL), BF16), pltpu.VMEM((TM_FULL, D_MODEL), BF16),
                        pltpu.VMEM((N_SG_HEADS, CHUNK, CHUNK), BF16)],
        compiler_params=_params("arbitrary"),
        name="sg_mix",
    )(up, vp, us, vs, _vec(ln_g), _vec(ln_b), w_s, bs_map, ws0_map, bs0_map, w_o, xp, xs)


def kernel(x_prompt, x_sample, state_conv_a, state_pool, state_short_conv, norm_mix, norm_mlp, norm_final, a_w_pw1, a_b_pw1, a_w_dw, a_b_dw, a_ln_g, a_ln_b, a_w_pw2, b_w_grp, b_scale, c_w_in, c_w_conv, c_w_out, d_w_uv, d_ln_g, d_ln_b, d_w_s, d_b_s, d_w_o, mlp_w1, mlp_w2):
    b, l, _ = x_prompt.shape
    mp = b * l
    ms = x_sample.shape[0]
    assert x_sample.shape[1] == 1 and mp % TM == 0 and l % TT == 0 and l % TT_POOL == 0
    flat3 = lambda a: a.reshape(b, l, D_MODEL)
    xp = x_prompt.reshape(mp, D_MODEL)
    xs = x_sample.reshape(ms, D_MODEL)
    tf = 512

    (gp,), (gs,), (w_pw2,) = _mm_parts(
        "a_pw1_glu", xp, xs, norm_mix[0], a_w_pw1.astype(BF16), 0, a_b_pw1[0], 2, (F32,),
        _glu_epilogue, 1024, cast_jobs=[(a_w_pw2, 0)])
    cp3, (w1_0, w2_0) = _conv_a_prompt(flat3(gp), a_w_dw[0], a_b_dw[0], a_ln_g[0], a_ln_b[0],
                                       cast_jobs=[(mlp_w1, 0), (mlp_w2, 0)])
    cp = cp3.reshape(mp, D_MODEL)
    cs, sa = _conv_a_sample(state_conv_a, 0, gs, a_w_dw[0], a_b_dw[0], a_ln_g[0], a_ln_b[0], TB_STATE)
    pa = flat3(gp)[:, l - (CONV_A_WIDTH - 1):]
    xp, xs = _mm_res("a_pw2", cp, cs, w_pw2, xp, xs)
    xp, xs, (w1_1, w2_1, w_in, w_out) = _mlp(
        "mlp0", xp, xs, norm_mlp[0], w1_0, w2_0, None, tf,
        cast_jobs=[(mlp_w1, 1), (mlp_w2, 1), (c_w_in, 0), (c_w_out, 0)])

    xp3, hs = _pool_prompt(flat3(xp), norm_mix[1], b_w_grp, 0, b_scale[0])
    xp = xp3.reshape(mp, D_MODEL)
    pb = hs[:, -1, POOL_HALO - POOL_STATE:]
    xs, sb = _pool_sample(xs, state_pool, norm_mix[1], b_w_grp, 0, b_scale[0], TB_STATE)
    xp, xs, (w1_2, w2_2, w_uv, w_o) = _mlp(
        "mlp1", xp, xs, norm_mlp[1], w1_1, w2_1, None, tf,
        cast_jobs=[(mlp_w1, 2), (mlp_w2, 2), (d_w_uv, 0), (d_w_o, 0)])

    zp, cx_tail, bgs, cxs = _c_in_conv(xp, xs, norm_mix[2], w_in, c_w_conv[0], l, 512)
    zs, sc = _conv_c_sample(bgs, cxs, state_short_conv, 0, c_w_conv[0])
    pc = cx_tail.reshape(b, l // TM, SUBLANES, D_MODEL)[:, -1, SUBLANES - (CONV_C_WIDTH - 1):]
    xp, xs = _mm_res("c_out", zp, zs, w_out, xp, xs)
    xp, xs, (w1_3, w2_3) = _mlp("mlp2", xp, xs, norm_mlp[2], w1_2, w2_2, None, tf,
                                cast_jobs=[(mlp_w1, 3), (mlp_w2, 3)])

    (up, vp), (us, vs), _ = _mm_parts("d_uv_gelu", xp, xs, norm_mix[3], w_uv, None, None, 2,
                                      (BF16, F32), _gelu_epilogue, 1024)
    bs_map = jnp.repeat(d_b_s[0].T, SG_HEAD_DIM, axis=1)
    ws0_map = _vec(jnp.repeat(d_w_s[0, :, 0, 0], SG_HEAD_DIM))
    bs0_map = _vec(jnp.repeat(d_b_s[0, :, 0], SG_HEAD_DIM))
    xp, xs, sd = _sg(up, vp, us, vs, d_ln_g[0], d_ln_b[0], d_w_s, 0, bs_map, ws0_map, bs0_map,
                     w_o, xp, xs)
    yp, ys, _ = _mlp("mlp3", xp, xs, norm_mlp[3], w1_3, w2_3, None, tf, final_g=norm_final)

    return (yp.reshape(b, l, D_MODEL), ys.reshape(ms, 1, D_MODEL),
            pa[None], sa, pb[None], sb, pc[None], sc,
            sd.reshape(1, ms, 1, D_MODEL))
```

```python
import functools

import jax
import jax.numpy as jnp
from jax import lax
from jax.experimental import pallas as pl
from jax.experimental.pallas import tpu as pltpu

D_MODEL = 2048
D_FF = 4 * D_MODEL
CONV_A_WIDTH = 31
POOL_WINDOWS = (2, 4, 8, 16)
POOL_GROUP = D_MODEL // len(POOL_WINDOWS)
POOL_STATE = max(POOL_WINDOWS) - 1
CONV_C_WIDTH = 3
CHUNK = 128
N_SG_HEADS = 8
SG_HEAD_DIM = D_MODEL // N_SG_HEADS
PAST_LEN = 16384
RMS_EPS = 1e-6
LN_EPS = 1e-5

SUBLANES = 8
BF16_ROWS = 16
VMEM_LIMIT_BYTES = 60000 * 1024
HALO = 32

TM = 1024
TM_FULL = 512
TT = 256
TT_POOL = 512
NORM_ROWS = 32
NORM_UNROLL = 4

F32 = jnp.float32
BF16 = jnp.bfloat16


def _params(*sem):
    return pltpu.CompilerParams(dimension_semantics=sem, vmem_limit_bytes=VMEM_LIMIT_BYTES)


def _for_chunks(n, chunk, body, unroll=1):
    assert n % chunk == 0
    if n == chunk:
        body(0)
    else:
        def step(i, c):
            body(pl.multiple_of(i * chunk, chunk))
            return c
        lax.fori_loop(0, n // chunk, step, 0, unroll=unroll)


def _rms_rows(x, g):
    r = lax.rsqrt(jnp.mean(x * x, axis=-1, keepdims=True) + RMS_EPS)
    return x * r * g


def _ln_rows(x, g, b):
    mu = jnp.mean(x, axis=-1, keepdims=True)
    xc = x - mu
    var = jnp.mean(xc * xc, axis=-1, keepdims=True)
    return xc * lax.rsqrt(var + LN_EPS) * g + b


def _row_chunk(tm):
    for c in (64, 32, 16, 8):
        if tm % c == 0:
            return c
    raise ValueError(tm)


def _rms_to_bf16(x_ref, g_ref, xn_ref):
    n = x_ref.shape[0]
    rc = _row_chunk(n)

    def body(r0):
        xn_ref[pl.ds(r0, rc), :] = _rms_rows(x_ref[pl.ds(r0, rc), :], g_ref[...]).astype(BF16)
    _for_chunks(n, rc, body, unroll=2)


def _vec(v):
    return v.reshape(1, D_MODEL)


def _sample_col(i, j, ni):
    return jnp.where(i == ni - 1, j, 0)


def _rows_ahead(ni, nj):
    j0 = max(1, nj // 2)
    return lambda i, j: (jnp.minimum(i + (j >= j0).astype(jnp.int32), ni - 1), 0)


def _wspec(block, index, layer, **kw):
    if layer is None:
        return pl.BlockSpec(block, index, **kw)
    return pl.BlockSpec((None,) + block, lambda *g: (layer,) + tuple(index(*g)), **kw)


def _cast_job_specs(jobs, n_steps, step_of):
    in_specs, args, out_specs, out_shapes = [], [], [], []
    for src, layer in jobs:
        _, r, c = src.shape
        assert r % (n_steps * BF16_ROWS) == 0, (src.shape, n_steps)
        pr = r // n_steps
        in_specs.append(pl.BlockSpec((None, pr, c),
                                     lambda *g, layer=layer: (layer, step_of(*g), 0)))
        args.append(src)
        out_specs.append(pl.BlockSpec((pr, c), lambda *g: (step_of(*g), 0)))
        out_shapes.append(jax.ShapeDtypeStruct((r, c), BF16))
    return in_specs, args, out_specs, out_shapes


def _run_cast_jobs(src_refs, dst_refs):
    for s_ref, d_ref in zip(src_refs, dst_refs):
        d_ref[...] = s_ref[...].astype(BF16)


def _mm_parts_kernel(*refs, n_parts, n_out, has_bias, n_jobs, epilogue):
    xp_ref, xs_ref, g_ref = refs[:3]
    w_refs = refs[3:3 + n_parts]
    pos = 3 + n_parts
    b_refs = refs[pos:pos + n_parts] if has_bias else ()
    pos += n_parts if has_bias else 0
    job_src = refs[pos:pos + n_jobs]
    pos += n_jobs
    op_refs = refs[pos:pos + n_out]
    os_refs = refs[pos + n_out:pos + 2 * n_out]
    pos += 2 * n_out
    job_dst = refs[pos:pos + n_jobs]
    xnp_ref, xns_ref = refs[pos + n_jobs:]
    i, j = pl.program_id(0), pl.program_id(1)
    last_i = pl.num_programs(0) - 1

    @pl.when(j == 0)
    def _():
        _rms_to_bf16(xp_ref, g_ref, xnp_ref)

    @pl.when((j == 0) & (i == last_i))
    def _():
        _rms_to_bf16(xs_ref, g_ref, xns_ref)

    def rows(xn_ref, o_refs):
        parts = []
        for p in range(n_parts):
            a = jnp.dot(xn_ref[...], w_refs[p][...], preferred_element_type=F32)
            if has_bias:
                a = a + b_refs[p][...]
            parts.append(a)
        for o_ref, o in zip(o_refs, epilogue(*parts)):
            o_ref[...] = o.astype(o_ref.dtype)

    rows(xnp_ref, op_refs)
    _run_cast_jobs(job_src, job_dst)

    @pl.when(i == last_i)
    def _():
        rows(xns_ref, os_refs)


def _mm_parts(name, xp, xs, g, w, layer, bias, n_parts, out_dtypes, epilogue, tn, cast_jobs=()):
    mp, ms = xp.shape[0], xs.shape[0]
    nj = D_MODEL // tn
    ni = mp // TM
    in_specs = [pl.BlockSpec((TM, D_MODEL), _rows_ahead(ni, nj)),
                pl.BlockSpec((ms, D_MODEL), lambda i, j: (0, 0)),
                pl.BlockSpec((1, D_MODEL), lambda i, j: (0, 0))]
    args = [xp, xs, _vec(g)]
    for p in range(n_parts):
        in_specs.append(_wspec((D_MODEL, tn), lambda i, j, p=p: (0, p * nj + j), layer))
        args.append(w)
    if bias is not None:
        for p in range(n_parts):
            in_specs.append(pl.BlockSpec((1, tn), lambda i, j, p=p: (0, p * nj + j)))
            args.append(bias.reshape(1, n_parts * D_MODEL))
    j_in, j_args, j_out, j_shapes = _cast_job_specs(cast_jobs, ni * nj, lambda i, j: i * nj + j)
    kern = functools.partial(_mm_parts_kernel, n_parts=n_parts, n_out=len(out_dtypes),
                             has_bias=bias is not None, n_jobs=len(cast_jobs), epilogue=epilogue)
    outs = pl.pallas_call(
        kern,
        grid=(ni, nj),
        in_specs=in_specs + j_in,
        out_specs=([pl.BlockSpec((TM, tn), lambda i, j: (i, j)) for _ in out_dtypes]
                   + [pl.BlockSpec((ms, tn), lambda i, j: (0, _sample_col(i, j, ni)))
                      for _ in out_dtypes] + j_out),
        out_shape=([jax.ShapeDtypeStruct((mp, D_MODEL), dt) for dt in out_dtypes]
                   + [jax.ShapeDtypeStruct((ms, D_MODEL), dt) for dt in out_dtypes] + j_shapes),
        scratch_shapes=[pltpu.VMEM((TM, D_MODEL), BF16), pltpu.VMEM((ms, D_MODEL), BF16)],
        compiler_params=_params("arbitrary", "arbitrary"),
        name=name,
    )(*args, *j_args)
    n = len(out_dtypes)
    return outs[:n], outs[n:2 * n], outs[2 * n:]


def _glu_epilogue(a, b):
    return (a * jax.nn.sigmoid(b),)


def _gelu_epilogue(a, b):
    return (jax.nn.gelu(a), jax.nn.gelu(b))


def _mm_res_kernel(ap_ref, as_ref, w_ref, xp_ref, xs_ref, op_ref, os_ref):
    op_ref[...] = xp_ref[...] + jnp.dot(ap_ref[...], w_ref[...], preferred_element_type=F32)

    @pl.when(pl.program_id(0) == pl.num_programs(0) - 1)
    def _():
        os_ref[...] = xs_ref[...] + jnp.dot(as_ref[...], w_ref[...], preferred_element_type=F32)


def _mm_res(name, ap, as_, w, xp, xs):
    mp, ms = ap.shape[0], as_.shape[0]
    rowp = pl.BlockSpec((TM_FULL, D_MODEL), lambda i: (i, 0))
    rows = pl.BlockSpec((ms, D_MODEL), lambda i: (0, 0))
    return pl.pallas_call(
        _mm_res_kernel,
        grid=(mp // TM_FULL,),
        in_specs=[rowp, rows,
                  pl.BlockSpec((D_MODEL, D_MODEL), lambda i: (0, 0), pipeline_mode=pl.Buffered(1)),
                  rowp, rows],
        out_specs=[rowp, rows],
        out_shape=[jax.ShapeDtypeStruct((mp, D_MODEL), F32),
                   jax.ShapeDtypeStruct((ms, D_MODEL), F32)],
        compiler_params=_params("arbitrary"),
        name=name,
    )(ap, as_, w, xp, xs)


def _mlp_kernel(*refs, final, n_jobs):
    n_in = 6 if final else 5
    xp_ref, xs_ref, g_ref, w1_ref, w2_ref = refs[:5]
    gf_ref = refs[5] if final else None
    job_src = refs[n_in:n_in + n_jobs]
    op_ref, os_ref = refs[n_in + n_jobs:n_in + n_jobs + 2]
    job_dst = refs[n_in + n_jobs + 2:n_in + 2 * n_jobs + 2]
    (xn_ref,) = refs[n_in + 2 * n_jobs + 2:]
    i, j = pl.program_id(0), pl.program_id(1)
    last_i = pl.num_programs(0) - 1
    last_j = pl.num_programs(1) - 1
    tm, ms = xp_ref.shape[0], xs_ref.shape[0]

    @pl.when(j == 0)
    def _():
        _rms_to_bf16(xp_ref, g_ref, xn_ref.at[pl.ds(0, tm)])

    @pl.when((j == 0) & (i == last_i))
    def _():
        _rms_to_bf16(xs_ref, g_ref, xn_ref.at[pl.ds(tm, ms)])

    def step(with_sample, first):
        xn = xn_ref[...] if with_sample else xn_ref[pl.ds(0, tm), :]
        h = jnp.dot(xn, w1_ref[...], preferred_element_type=F32)
        h = jnp.square(jnp.maximum(h, 0.0)).astype(BF16)
        y = jnp.dot(h, w2_ref[...], preferred_element_type=F32)
        outs = [(xp_ref, op_ref, y[:tm] if with_sample else y)]
        if with_sample:
            outs.append((xs_ref, os_ref, y[tm:]))
        for x_ref, o_ref, yy in outs:
            if first:
                o_ref[...] = x_ref[...] + yy
            else:
                o_ref[...] += yy
        _run_cast_jobs(job_src, job_dst)

    for with_sample in (False, True):
        for first in (True, False):
            cond = ((i == last_i) if with_sample else (i != last_i)) & ((j == 0) if first else (j > 0))
            pl.when(cond)(functools.partial(step, with_sample, first))

    if final:
        def norm(o_ref):
            n = o_ref.shape[0]
            rc = _row_chunk(n)

            def body(r0):
                o_ref[pl.ds(r0, rc), :] = _rms_rows(o_ref[pl.ds(r0, rc), :], gf_ref[...])
            _for_chunks(n, rc, body, unroll=2)

        @pl.when(j == last_j)
        def _():
            norm(op_ref)

        @pl.when((j == last_j) & (i == last_i))
        def _():
            norm(os_ref)


def _mlp(name, xp, xs, g, w1, w2, layer, tf, final_g=None, cast_jobs=()):
    mp, ms = xp.shape[0], xs.shape[0]
    final = final_g is not None
    ni, nj = mp // TM, D_FF // tf
    in_specs = [pl.BlockSpec((TM, D_MODEL), _rows_ahead(ni, nj)),
                pl.BlockSpec((ms, D_MODEL), lambda i, j: (0, 0)),
                pl.BlockSpec((1, D_MODEL), lambda i, j: (0, 0)),
                _wspec((D_MODEL, tf), lambda i, j: (0, j), layer),
                _wspec((tf, D_MODEL), lambda i, j: (j, 0), layer)]
    args = [xp, xs, _vec(g), w1, w2]
    if final:
        in_specs.append(pl.BlockSpec((1, D_MODEL), lambda i, j: (0, 0)))
        args.append(_vec(final_g))
    j_in, j_args, j_out, j_shapes = _cast_job_specs(cast_jobs, ni * nj, lambda i, j: i * nj + j)
    outs = pl.pallas_call(
        functools.partial(_mlp_kernel, final=final, n_jobs=len(cast_jobs)),
        grid=(ni, nj),
        in_specs=in_specs + j_in,
        out_specs=[pl.BlockSpec((TM, D_MODEL), lambda i, j: (i, 0)),
                   pl.BlockSpec((ms, D_MODEL), lambda i, j: (0, 0))] + j_out,
        out_shape=[jax.ShapeDtypeStruct((mp, D_MODEL), F32),
                   jax.ShapeDtypeStruct((ms, D_MODEL), F32)] + j_shapes,
        scratch_shapes=[pltpu.VMEM((TM + ms, D_MODEL), BF16)],
        compiler_params=_params("arbitrary", "arbitrary"),
        name=name,
    )(*args, *j_args)
    return outs[0], outs[1], outs[2:]


CONV_ROWS = 128
CONV_LANES = 256


def _carry_halo(ext_ref, t, tt, halo):
    @pl.when(t == 0)
    def _():
        ext_ref[pl.ds(0, halo), :] = jnp.zeros((halo, D_MODEL), F32)

    @pl.when(t > 0)
    def _():
        ext_ref[pl.ds(0, halo), :] = ext_ref[pl.ds(tt, halo), :]


def _conv_a_kernel(*refs, tt, n_jobs):
    g_ref, w_ref, bdw_ref, lng_ref, lnb_ref = refs[:5]
    job_src = refs[5:5 + n_jobs]
    o_ref = refs[5 + n_jobs]
    job_dst = refs[6 + n_jobs:6 + 2 * n_jobs]
    ext_ref, c_ref = refs[6 + 2 * n_jobs:]
    _run_cast_jobs(job_src, job_dst)
    _carry_halo(ext_ref, pl.program_id(1), tt, HALO)

    def copy_in(r0):
        ext_ref[pl.ds(HALO + r0, 64), :] = g_ref[0, pl.ds(r0, 64), :]
    _for_chunks(tt, 64, copy_in)

    off = HALO - (CONV_A_WIDTH - 1)
    win = CONV_ROWS + HALO
    tiles = CONV_ROWS // SUBLANES
    for c0 in range(0, D_MODEL, CONV_LANES):
        lanes = pl.ds(c0, CONV_LANES)

        def rows(r0, lanes=lanes):
            w = ext_ref[pl.ds(r0, win), lanes]
            acc = jnp.zeros((tiles, SUBLANES, CONV_LANES), F32)
            for s in range(SUBLANES):
                ws = w if s == 0 else pltpu.roll(w, win - s, 0)
                for k in range(CONV_A_WIDTH):
                    if (off + k) % SUBLANES == s:
                        a = off + k - s
                        tap = ws[a:a + CONV_ROWS].reshape(tiles, SUBLANES, CONV_LANES)
                        acc = acc + tap * w_ref[k, :, lanes][None]
            c_ref[pl.ds(r0, CONV_ROWS), lanes] = acc.reshape(CONV_ROWS, CONV_LANES)
        _for_chunks(tt, CONV_ROWS, rows)

    def norm(r0):
        c = c_ref[pl.ds(r0, NORM_ROWS), :] + bdw_ref[...]
        y = _ln_rows(c, lng_ref[...], lnb_ref[...])
        o_ref[0, pl.ds(r0, NORM_ROWS), :] = (y * jax.nn.sigmoid(y)).astype(BF16)
    _for_chunks(tt, NORM_ROWS, norm, unroll=NORM_UNROLL)


def _pad_rows(w, rows):
    return jnp.zeros((rows, D_MODEL), F32).at[:w.shape[0]].set(w)


def _conv_a_prompt(g3, w_dw, b_dw, ln_g, ln_b, cast_jobs=()):
    b, l, _ = g3.shape
    nt = l // TT
    vspec = pl.BlockSpec((1, D_MODEL), lambda i, t: (0, 0))
    w_rep = jnp.broadcast_to(_pad_rows(w_dw, 32)[:, None, :], (32, SUBLANES, D_MODEL))
    j_in, j_args, j_out, j_shapes = _cast_job_specs(cast_jobs, b * nt, lambda i, t: i * nt + t)
    outs = pl.pallas_call(
        functools.partial(_conv_a_kernel, tt=TT, n_jobs=len(cast_jobs)),
        grid=(b, nt),
        in_specs=[pl.BlockSpec((1, TT, D_MODEL), lambda i, t: (i, t, 0)),
                  pl.BlockSpec((32, SUBLANES, D_MODEL), lambda i, t: (0, 0, 0)),
                  vspec, vspec, vspec] + j_in,
        out_specs=[pl.BlockSpec((1, TT, D_MODEL), lambda i, t: (i, t, 0))] + j_out,
        out_shape=[jax.ShapeDtypeStruct((b, l, D_MODEL), BF16)] + j_shapes,
        scratch_shapes=[pltpu.VMEM((TT + HALO, D_MODEL), F32),
                        pltpu.VMEM((TT, D_MODEL), F32)],
        compiler_params=_params("arbitrary", "arbitrary"),
        name="conv_a_prompt",
    )(g3, w_rep, _vec(b_dw), _vec(ln_g), _vec(ln_b), *j_args)
    return outs[0], outs[1:]


def _rows_major(state):
    return jnp.transpose(state, (0, 2, 1, 3))


def _push_state(st_ref, new_row, ns_ref):
    n_st = st_ref.shape[0]
    for k in range(n_st - 1):
        ns_ref[k] = st_ref[k + 1]
    ns_ref[n_st - 1] = new_row


def _state_specs(n_st, tb, layer):
    return (pl.BlockSpec((None, n_st, tb, D_MODEL), lambda i: (layer, 0, i, 0)),
            pl.BlockSpec((None, n_st, tb, D_MODEL), lambda i: (0, 0, i, 0)))


def _conv_a_sample_kernel(st_ref, g_ref, w_ref, bdw_ref, lng_ref, lnb_ref, o_ref, ns_ref):
    n_st = CONV_A_WIDTH - 1
    acc = g_ref[...] * w_ref[pl.ds(n_st, 1), :]
    for k in range(n_st):
        acc = acc + st_ref[k] * w_ref[pl.ds(k, 1), :]
    y = _ln_rows(acc + bdw_ref[...], lng_ref[...], lnb_ref[...])
    o_ref[...] = (y * jax.nn.sigmoid(y)).astype(BF16)
    _push_state(st_ref, g_ref[...], ns_ref)


def _conv_a_sample(state, layer, g, w_dw, b_dw, ln_g, ln_b, tb):
    n = g.shape[0]
    n_st = CONV_A_WIDTH - 1
    vspec = pl.BlockSpec((1, D_MODEL), lambda i: (0, 0))
    st_in, st_out = _state_specs(n_st, tb, layer)
    out, ns = pl.pallas_call(
        _conv_a_sample_kernel,
        grid=(n // tb,),
        in_specs=[st_in,
                  pl.BlockSpec((tb, D_MODEL), lambda i: (i, 0)),
                  pl.BlockSpec((32, D_MODEL), lambda i: (0, 0)),
                  vspec, vspec, vspec],
        out_specs=[pl.BlockSpec((tb, D_MODEL), lambda i: (i, 0)), st_out],
        out_shape=[jax.ShapeDtypeStruct((n, D_MODEL), BF16),
                   jax.ShapeDtypeStruct((1, n_st, n, D_MODEL), F32)],
        compiler_params=_params("arbitrary"),
        name="conv_a_sample",
    )(_rows_major(state), g, _pad_rows(w_dw, 32), _vec(b_dw), _vec(ln_g), _vec(ln_b))
    return out, _rows_major(ns)


POOL_HALO = 16


def _pool_prompt_kernel(x_ref, g_ref, w_ref, sc_ref, o_ref, hs_ref, ext_ref, diff_ref, wb_ref,
                        *, tt):
    t = pl.program_id(1)
    H = POOL_HALO
    _carry_halo(ext_ref, t, tt, H)

    def norm(r0):
        ext_ref[pl.ds(H + r0, NORM_ROWS), :] = _rms_rows(x_ref[0, pl.ds(r0, NORM_ROWS), :],
                                                         g_ref[...])
    _for_chunks(tt, NORM_ROWS, norm, unroll=NORM_UNROLL)

    hs_ref[0, 0] = ext_ref[pl.ds(tt, H), :]

    @pl.when((pl.program_id(0) == 0) & (t == 0))
    def _():
        wb_ref[...] = w_ref[...].astype(BF16)

    pr = 64
    win = pr + H
    for gi, w in enumerate(POOL_WINDOWS):
        lanes = pl.ds(gi * POOL_GROUP, POOL_GROUP)

        def rows(r0, w=w, lanes=lanes):
            s = ext_ref[pl.ds(r0, win), lanes]
            cur = s[H:]
            d = 1
            while d < w:
                s = s + pltpu.roll(s, d, 0)
                d *= 2
            pos = t * tt + r0 + lax.broadcasted_iota(jnp.int32, (pr, 1), 0)
            cnt = jnp.minimum(pos + 1, w).astype(F32)
            diff_ref[pl.ds(r0, pr), lanes] = (s[H:] / cnt - cur).astype(BF16)
        _for_chunks(tt, pr, rows)

    for gi in range(len(POOL_WINDOWS)):
        sl = slice(gi * POOL_GROUP, (gi + 1) * POOL_GROUP)
        y = jnp.dot(diff_ref[:, sl], wb_ref[gi], preferred_element_type=F32)
        o_ref[0, :, sl] = x_ref[0, :, sl] + y * sc_ref[:, sl]


def _pool_prompt(x3, g, w_grp, layer, scale):
    b, l, _ = x3.shape
    tt = TT_POOL
    nt = l // tt
    ng = len(POOL_WINDOWS)
    vspec = pl.BlockSpec((1, D_MODEL), lambda i, t: (0, 0))
    return pl.pallas_call(
        functools.partial(_pool_prompt_kernel, tt=tt),
        grid=(b, nt),
        in_specs=[pl.BlockSpec((1, tt, D_MODEL), lambda i, t: (i, t, 0)),
                  vspec,
                  pl.BlockSpec((None, ng, POOL_GROUP, POOL_GROUP), lambda i, t: (layer, 0, 0, 0)),
                  vspec],
        out_specs=[pl.BlockSpec((1, tt, D_MODEL), lambda i, t: (i, t, 0)),
                   pl.BlockSpec((1, 1, POOL_HALO, D_MODEL), lambda i, t: (i, t, 0, 0))],
        out_shape=[jax.ShapeDtypeStruct((b, l, D_MODEL), F32),
                   jax.ShapeDtypeStruct((b, nt, POOL_HALO, D_MODEL), F32)],
        scratch_shapes=[pltpu.VMEM((tt + POOL_HALO, D_MODEL), F32),
                        pltpu.VMEM((tt, D_MODEL), BF16),
                        pltpu.VMEM((ng, POOL_GROUP, POOL_GROUP), BF16)],
        compiler_params=_params("arbitrary", "arbitrary"),
        name="pool_prompt",
    )(x3, _vec(g), w_grp, _vec(scale))


def _pool_sample_kernel(x_ref, st_ref, g_ref, w_ref, sc_ref, o_ref, ns_ref):
    x = x_ref[...]
    h = _rms_rows(x, g_ref[...])
    for gi, w in enumerate(POOL_WINDOWS):
        sl = slice(gi * POOL_GROUP, (gi + 1) * POOL_GROUP)
        cur = h[:, sl]
        s = cur
        for d in range(1, w):
            s = s + st_ref[POOL_STATE - d, :, sl]
        diff = s / float(min(PAST_LEN + 1, w)) - cur
        y = jnp.dot(diff.astype(BF16), w_ref[gi].astype(BF16), preferred_element_type=F32)
        o_ref[:, sl] = x[:, sl] + y * sc_ref[:, sl]
    _push_state(st_ref, h, ns_ref)


def _pool_sample(x, state, g, w_grp, layer, scale, tb):
    n = x.shape[0]
    ng = len(POOL_WINDOWS)
    vspec = pl.BlockSpec((1, D_MODEL), lambda i: (0, 0))
    row = pl.BlockSpec((tb, D_MODEL), lambda i: (i, 0))
    st_in, st_out = _state_specs(POOL_STATE, tb, layer)
    out, ns = pl.pallas_call(
        _pool_sample_kernel,
        grid=(n // tb,),
        in_specs=[row, st_in, vspec,
                  pl.BlockSpec((None, ng, POOL_GROUP, POOL_GROUP), lambda i: (layer, 0, 0, 0)),
                  vspec],
        out_specs=[row, st_out],
        out_shape=[jax.ShapeDtypeStruct((n, D_MODEL), F32),
                   jax.ShapeDtypeStruct((1, POOL_STATE, n, D_MODEL), F32)],
        compiler_params=_params("arbitrary"),
        name="pool_sample",
    )(x, _rows_major(state), _vec(g), w_grp, _vec(scale))
    return out, _rows_major(ns)


C_ROWS = 128


def _c_in_kernel(xp_ref, xs_ref, g_ref, wb_ref, wc_ref, wx_ref, wconv_ref,
                 z_ref, tail_ref, bgs_ref, cxs_ref, xnp_ref, xns_ref, ext_ref, carry_ref,
                 *, blocks_per_seq):
    i, j = pl.program_id(0), pl.program_id(1)
    last_i = pl.num_programs(0) - 1
    H = SUBLANES
    tm = xp_ref.shape[0]

    @pl.when(j == 0)
    def _():
        _rms_to_bf16(xp_ref, g_ref, xnp_ref)

    @pl.when((j == 0) & (i == last_i))
    def _():
        _rms_to_bf16(xs_ref, g_ref, xns_ref)

    def parts(xn_ref):
        xn = xn_ref[...]
        return tuple(jnp.dot(xn, w_ref[...], preferred_element_type=F32)
                     for w_ref in (wb_ref, wc_ref, wx_ref))

    bg, c, xv = parts(xnp_ref)
    cx = c * xv
    prev = carry_ref[j]
    ext_ref[pl.ds(0, H), :] = jnp.where(i % blocks_per_seq == 0, jnp.zeros_like(prev), prev)
    ext_ref[pl.ds(H, tm), :] = cx
    carry_ref[j] = cx[tm - H:]
    tail_ref[0] = cx[tm - H:]
    win = C_ROWS + H
    for r0 in range(0, tm, C_ROWS):
        w = ext_ref[pl.ds(r0, win), :]
        y = (pltpu.roll(w, 2, 0)[H:] * wconv_ref[pl.ds(0, 1), :]
             + pltpu.roll(w, 1, 0)[H:] * wconv_ref[pl.ds(1, 1), :]
             + w[H:] * wconv_ref[pl.ds(2, 1), :])
        z_ref[pl.ds(r0, C_ROWS), :] = (bg[r0:r0 + C_ROWS] * y).astype(BF16)

    @pl.when(i == last_i)
    def _():
        bs, cs, xvs = parts(xns_ref)
        bgs_ref[...] = bs
        cxs_ref[...] = cs * xvs


def _c_in_conv(xp, xs, g, w, w_conv, seq_len, tn):
    mp, ms = xp.shape[0], xs.shape[0]
    nj = D_MODEL // tn
    ni = mp // TM
    assert seq_len % TM == 0
    wspecs = [pl.BlockSpec((D_MODEL, tn), lambda i, j, p=p: (0, p * nj + j)) for p in range(3)]
    tile_s = pl.BlockSpec((ms, tn), lambda i, j: (0, _sample_col(i, j, ni)))
    return pl.pallas_call(
        functools.partial(_c_in_kernel, blocks_per_seq=seq_len // TM),
        grid=(ni, nj),
        in_specs=[pl.BlockSpec((TM, D_MODEL), _rows_ahead(ni, nj)),
                  pl.BlockSpec((ms, D_MODEL), lambda i, j: (0, 0)),
                  pl.BlockSpec((1, D_MODEL), lambda i, j: (0, 0))] + wspecs
                 + [pl.BlockSpec((SUBLANES, tn), lambda i, j: (0, j))],
        out_specs=[pl.BlockSpec((TM, tn), lambda i, j: (i, j)),
                   pl.BlockSpec((1, SUBLANES, tn), lambda i, j: (i, 0, j)),
                   tile_s, tile_s],
        out_shape=[jax.ShapeDtypeStruct((mp, D_MODEL), BF16),
                   jax.ShapeDtypeStruct((ni, SUBLANES, D_MODEL), F32),
                   jax.ShapeDtypeStruct((ms, D_MODEL), F32),
                   jax.ShapeDtypeStruct((ms, D_MODEL), F32)],
        scratch_shapes=[pltpu.VMEM((TM, D_MODEL), BF16), pltpu.VMEM((ms, D_MODEL), BF16),
                        pltpu.VMEM((TM + SUBLANES, tn), F32),
                        pltpu.VMEM((nj, SUBLANES, tn), F32)],
        compiler_params=_params("arbitrary", "arbitrary"),
        name="c_in_conv",
    )(xp, xs, _vec(g), w, w, w, _pad_rows(w_conv, SUBLANES))


def _conv_c_sample_kernel(b_ref, cx_ref, st_ref, w_ref, o_ref, ns_ref):
    y = (st_ref[0] * w_ref[pl.ds(0, 1), :] + st_ref[1] * w_ref[pl.ds(1, 1), :]
         + cx_ref[...] * w_ref[pl.ds(2, 1), :])
    o_ref[...] = (b_ref[...] * y).astype(BF16)
    _push_state(st_ref, cx_ref[...], ns_ref)


def _conv_c_sample(bg, cx, state, layer, w_conv):
    n = bg.shape[0]
    n_st = CONV_C_WIDTH - 1
    row = pl.BlockSpec((n, D_MODEL), lambda i: (0, 0))
    st_in, st_out = _state_specs(n_st, n, layer)
    out, ns = pl.pallas_call(
        _conv_c_sample_kernel,
        grid=(1,),
        in_specs=[row, row, st_in, pl.BlockSpec((SUBLANES, D_MODEL), lambda i: (0, 0))],
        out_specs=[row, st_out],
        out_shape=[jax.ShapeDtypeStruct((n, D_MODEL), BF16),
                   jax.ShapeDtypeStruct((1, n_st, n, D_MODEL), F32)],
        compiler_params=_params("arbitrary"),
        name="conv_c_sample",
    )(bg, cx, _rows_major(state), _pad_rows(w_conv, SUBLANES))
    return out, _rows_major(ns)


def _sg_kernel(up_ref, vp_ref, us_ref, vs_ref, lng_ref, lnb_ref, ws_ref, bs_ref, ws0_ref, bs0_ref,
               wo_ref, xp_ref, xs_ref, op_ref, os_ref, vns_ref, vn_ref, gt_ref, wsm_ref):
    i = pl.program_id(0)
    tm = up_ref.shape[0]

    @pl.when(i == 0)
    def _():
        row = lax.broadcasted_iota(jnp.int32, (CHUNK, CHUNK), 0)
        col = lax.broadcasted_iota(jnp.int32, (CHUNK, CHUNK), 1)
        for hd in range(N_SG_HEADS):
            wsm_ref[hd] = jnp.where(col <= row, ws_ref[hd], 0.0).astype(BF16)

    def norm(r0):
        vn_ref[pl.ds(r0, NORM_ROWS), :] = _ln_rows(vp_ref[pl.ds(r0, NORM_ROWS), :], lng_ref[...],
                                                   lnb_ref[...]).astype(BF16)
    _for_chunks(tm, NORM_ROWS, norm, unroll=NORM_UNROLL)

    for r0 in range(0, tm, CHUNK):
        for hd in range(N_SG_HEADS):
            lanes = pl.ds(hd * SG_HEAD_DIM, SG_HEAD_DIM)
            mixed = jnp.dot(wsm_ref[hd], vn_ref[pl.ds(r0, CHUNK), lanes],
                            preferred_element_type=F32) + bs_ref[:, lanes]
            gt_ref[pl.ds(r0, CHUNK), lanes] = (up_ref[pl.ds(r0, CHUNK), lanes].astype(F32)
                                               * mixed).astype(BF16)

    op_ref[...] = xp_ref[...] + jnp.dot(gt_ref[...], wo_ref[...], preferred_element_type=F32)

    @pl.when(i == pl.num_programs(0) - 1)
    def _():
        vn = _ln_rows(vs_ref[...], lng_ref[...], lnb_ref[...])
        vns_ref[...] = vn
        mixed = ws0_ref[...] * vn + bs0_ref[...]
        gated = (us_ref[...].astype(F32) * mixed).astype(BF16)
        os_ref[...] = xs_ref[...] + jnp.dot(gated, wo_ref[...], preferred_element_type=F32)


def _sg(up, vp, us, vs, ln_g, ln_b, w_s, layer, bs_map, ws0_map, bs0_map, w_o, xp, xs):
    mp, ms = up.shape[0], us.shape[0]
    vspec = pl.BlockSpec((1, D_MODEL), lambda i: (0, 0))
    rowp = pl.BlockSpec((TM_FULL, D_MODEL), lambda i: (i, 0))
    rows = pl.BlockSpec((ms, D_MODEL), lambda i: (0, 0))
    return pl.pallas_call(
        _sg_kernel,
        grid=(mp // TM_FULL,),
        in_specs=[rowp, rowp, rows, rows, vspec, vspec,
                  pl.BlockSpec((None, N_SG_HEADS, CHUNK, CHUNK), lambda i: (layer, 0, 0, 0)),
                  pl.BlockSpec((CHUNK, D_MODEL), lambda i: (0, 0)),
                  vspec, vspec,
                  pl.BlockSpec((D_MODEL, D_MODEL), lambda i: (0, 0), pipeline_mode=pl.Buffered(1)),
                  rowp, rows],
        out_specs=[rowp, rows, rows],
        out_shape=[jax.ShapeDtypeStruct((mp, D_MODEL), F32),
                   jax.ShapeDtypeStruct((ms, D_MODEL), F32),
                   jax.ShapeDtypeStruct((ms, D_MODEL), F32)],
        scratch_shapes=[pltpu.VMEM((TM_FULL, D_MODEL), BF16), pltpu.VMEM((TM_FULL, D_MODEL), BF16),
                        pltpu.VMEM((N_SG_HEADS, CHUNK, CHUNK), BF16)],
        compiler_params=_params("arbitrary"),
        name="sg_mix",
    )(up, vp, us, vs, _vec(ln_g), _vec(ln_b), w_s, bs_map, ws0_map, bs0_map, w_o, xp, xs)


def kernel(x_prompt, x_sample, state_conv_a, state_pool, state_short_conv, norm_mix, norm_mlp, norm_final, a_w_pw1, a_b_pw1, a_w_dw, a_b_dw, a_ln_g, a_ln_b, a_w_pw2, b_w_grp, b_scale, c_w_in, c_w_conv, c_w_out, d_w_uv, d_ln_g, d_ln_b, d_w_s, d_b_s, d_w_o, mlp_w1, mlp_w2):
    b, l, _ = x_prompt.shape
    mp = b * l
    ms = x_sample.shape[0]
    assert x_sample.shape[1] == 1 and mp % TM == 0 and l % TT == 0 and l % TT_POOL == 0
    flat3 = lambda a: a.reshape(b, l, D_MODEL)
    xp = x_prompt.reshape(mp, D_MODEL)
    xs = x_sample.reshape(ms, D_MODEL)
    tf = 512

    (gp,), (gs,), (w_pw2,) = _mm_parts(
        "a_pw1_glu", xp, xs, norm_mix[0], a_w_pw1.astype(BF16), 0, a_b_pw1[0], 2, (F32,),
        _glu_epilogue, 1024, cast_jobs=[(a_w_pw2, 0)])
    cp3, (w1_0, w2_0) = _conv_a_prompt(flat3(gp), a_w_dw[0], a_b_dw[0], a_ln_g[0], a_ln_b[0],
                                       cast_jobs=[(mlp_w1, 0), (mlp_w2, 0)])
    cp = cp3.reshape(mp, D_MODEL)
    cs, sa = _conv_a_sample(state_conv_a, 0, gs, a_w_dw[0], a_b_dw[0], a_ln_g[0], a_ln_b[0], 32)
    pa = flat3(gp)[:, l - (CONV_A_WIDTH - 1):]
    xp, xs = _mm_res("a_pw2", cp, cs, w_pw2, xp, xs)
    xp, xs, (w1_1, w2_1, w_in, w_out) = _mlp(
        "mlp0", xp, xs, norm_mlp[0], w1_0, w2_0, None, tf,
        cast_jobs=[(mlp_w1, 1), (mlp_w2, 1), (c_w_in, 0), (c_w_out, 0)])

    xp3, hs = _pool_prompt(flat3(xp), norm_mix[1], b_w_grp, 0, b_scale[0])
    xp = xp3.reshape(mp, D_MODEL)
    pb = hs[:, -1, POOL_HALO - POOL_STATE:]
    xs, sb = _pool_sample(xs, state_pool, norm_mix[1], b_w_grp, 0, b_scale[0], 32)
    xp, xs, (w1_2, w2_2, w_uv, w_o) = _mlp(
        "mlp1", xp, xs, norm_mlp[1], w1_1, w2_1, None, tf,
        cast_jobs=[(mlp_w1, 2), (mlp_w2, 2), (d_w_uv, 0), (d_w_o, 0)])

    zp, cx_tail, bgs, cxs = _c_in_conv(xp, xs, norm_mix[2], w_in, c_w_conv[0], l, 512)
    zs, sc = _conv_c_sample(bgs, cxs, state_short_conv, 0, c_w_conv[0])
    pc = cx_tail.reshape(b, l // TM, SUBLANES, D_MODEL)[:, -1, SUBLANES - (CONV_C_WIDTH - 1):]
    xp, xs = _mm_res("c_out", zp, zs, w_out, xp, xs)
    xp, xs, (w1_3, w2_3) = _mlp("mlp2", xp, xs, norm_mlp[2], w1_2, w2_2, None, tf,
                                cast_jobs=[(mlp_w1, 3), (mlp_w2, 3)])

    (up, vp), (us, vs), _ = _mm_parts("d_uv_gelu", xp, xs, norm_mix[3], w_uv, None, None, 2,
                                      (BF16, F32), _gelu_epilogue, 1024)
    bs_map = jnp.repeat(d_b_s[0].T, SG_HEAD_DIM, axis=1)
    ws0_map = _vec(jnp.repeat(d_w_s[0, :, 0, 0], SG_HEAD_DIM))
    bs0_map = _vec(jnp.repeat(d_b_s[0, :, 0], SG_HEAD_DIM))
    xp, xs, sd = _sg(up, vp, us, vs, d_ln_g[0], d_ln_b[0], d_w_s, 0, bs_map, ws0_map, bs0_map,
                     w_o, xp, xs)
    yp, ys, _ = _mlp("mlp3", xp, xs, norm_mlp[3], w1_3, w2_3, None, tf, final_g=norm_final)

    return (yp.reshape(b, l, D_MODEL), ys.reshape(ms, 1, D_MODEL),
            pa[None], sa, pb[None], sb, pc[None], sc,
            sd.reshape(1, ms, 1, D_MODEL))
```
